```python
import jax, jax.numpy as jnp
from jax import lax
import numpy as np

D_MODEL = 2048
BATCH = 4
SEQ = 4096
DEPTH = 4

CHUNK = 64
RWKV_WIDTH = D_MODEL // 2
RWKV_HEAD = 64
RWKV_HEADS = RWKV_WIDTH // RWKV_HEAD
CONV_WIDTH = D_MODEL - RWKV_WIDTH
SHORT_CONV = 3
DECAY_LORA = 64
AAA_LORA = 64
MV_LORA = 32
GATE_LORA = 160
RWKV_COLS = 3 * RWKV_WIDTH + DECAY_LORA + AAA_LORA + GATE_LORA
EVEN_COLS = RWKV_COLS + 3 * CONV_WIDTH
GN_EPS = 64e-5
FOX_HEADS = 16
FOX_HEAD = D_MODEL // FOX_HEADS
Q_BLOCK = 128
ODD_COLS = 3 * D_MODEL + FOX_HEADS
D_FF = 5632
FFN_CONV = 3
LN_EPS = 1e-5
N_EVEN = (DEPTH + 1) // 2
N_ODD = DEPTH // 2
DN_ALPHA = (2 * DEPTH) ** 0.25
DN_BETA = (8 * DEPTH) ** -0.25

kernel_name = 'hybrid_rwkv7_shortconv_fox_convffn_deepnorm'


def layer_norm(x, g, b):
    xf = x.astype(jnp.float32)
    mu = jnp.mean(xf, axis=-1, keepdims=True)
    var = jnp.mean(jnp.square(xf - mu), axis=-1, keepdims=True)
    return ((xf - mu) * lax.rsqrt(var + LN_EPS) * g + b).astype(x.dtype)


def causal_dwconv(u, w, b=None):
    K, C = w.shape
    y = lax.conv_general_dilated(u, w[:, None, :].astype(u.dtype), window_strides=(1,),
                                 padding=[(K - 1, 0)], dimension_numbers=('NWC', 'WIO', 'NWC'),
                                 feature_group_count=C)
    if b is not None:
        y = y + b
    return y


def token_shift(u):
    return jnp.pad(u, ((0, 0), (1, 0), (0, 0)))[:, :-1]


def rwkv7_scan(r, w, k, v, a_vec, b_vec):
    Bsz, _, H, N = r.shape

    def step(S, inp):
        r_t, w_t, k_t, v_t, a_t, b_t = inp
        sa = jnp.einsum('bhij,bhj->bhi', S, a_t)
        S = S * w_t[:, :, None, :] + sa[..., :, None] * b_t[..., None, :] + v_t[..., :, None] * k_t[..., None, :]
        return S, jnp.einsum('bhij,bhj->bhi', S, r_t)

    seq = tuple(jnp.moveaxis(t, 1, 0) for t in (r, w, k, v, a_vec, b_vec))
    S0 = jnp.zeros((Bsz, H, N, N), jnp.float32)
    _, ys = lax.scan(step, S0, seq)
    return jnp.moveaxis(ys, 0, 1)


def rwkv_conv_mixer(x, w_in, mu, w0, w2, a0, a2, g2, k_k, k_a, r_k, lnx_g, lnx_b,
                    conv_w, w_out, v_first, v_res):
    Bsz, T, _ = x.shape
    RW, H, N = RWKV_WIDTH, RWKV_HEADS, RWKV_HEAD
    f32 = jnp.float32
    proj = x @ w_in
    pr, pc = proj[..., :RWKV_COLS], proj[..., RWKV_COLS:]
    pr = pr + (token_shift(pr) - pr) * mu
    idx = [RW, 2 * RW, 3 * RW, 3 * RW + DECAY_LORA, 3 * RW + DECAY_LORA + AAA_LORA]
    r, k, v, dw, da, dg = jnp.split(pr, idx, axis=-1)
    w_log = -jax.nn.softplus(-(w0 + jnp.tanh(dw) @ w2).astype(f32)) - 0.5
    decay = jnp.exp(-jnp.exp(w_log))
    a = jax.nn.sigmoid(a0 + da @ a2)
    g = jax.nn.sigmoid(dg) @ g2
    if v_res is None:
        v_first = v
    else:
        v0, v1, v2 = v_res
        v = v + (v_first - v) * jax.nn.sigmoid(v0 + (v @ v1) @ v2)
    hd = lambda t: t.reshape(Bsz, T, H, N)
    kk = hd(k * k_k).astype(f32)
    kk = kk / jnp.maximum(jnp.sqrt(jnp.sum(kk * kk, axis=-1, keepdims=True)), 1e-12)
    k = k * (1 + (a - 1) * k_a)
    rh, kh, vh = hd(r).astype(f32), hd(k).astype(f32), hd(v).astype(f32)
    y = rwkv7_scan(rh, hd(decay), kh, vh, -kk, kk * hd(a).astype(f32))
    m = jnp.mean(y, axis=-1, keepdims=True)
    var = jnp.mean(jnp.square(y - m), axis=-1, keepdims=True)
    y = ((y - m) * lax.rsqrt(var + GN_EPS)).reshape(Bsz, T, RW) * lnx_g + lnx_b
    bonus = (jnp.sum(rh * kh * r_k, axis=-1, keepdims=True) * vh).reshape(Bsz, T, RW)
    y_rwkv = ((y + bonus) * g).astype(x.dtype)
    gb, gc, h = jnp.split(pc, 3, axis=-1)
    y_conv = gb * causal_dwconv(gc * h, conv_w)
    out = jnp.concatenate([y_rwkv, y_conv], axis=-1) @ w_out
    return out, v_first


def fox_mixer(x, w_in, b_f, w_out):
    Bsz, T, D = x.shape
    H, Dh = FOX_HEADS, FOX_HEAD
    proj = x @ w_in
    q, k, v, fl = jnp.split(proj, [D, 2 * D, 3 * D], axis=-1)
    q = q.reshape(Bsz, T, H, Dh)
    k = k.reshape(Bsz, T, H, Dh)
    v = v.reshape(Bsz, T, H, Dh)
    log_f = jax.nn.log_sigmoid((fl + b_f).astype(jnp.float32))
    c = jnp.moveaxis(jnp.cumsum(log_f, axis=1), 1, 2)
    scale = FOX_HEAD ** -0.5
    outs = []
    for start in range(0, T, Q_BLOCK):
        end = start + Q_BLOCK
        logits = jnp.einsum('bqhd,bkhd->bhqk', q[:, start:end], k[:, :end]).astype(jnp.float32) * scale
        logits = logits + (c[:, :, start:end, None] - c[:, :, None, :end])
        mask = jnp.arange(start, end)[:, None] >= jnp.arange(end)[None, :]
        logits = jnp.where(mask, logits, -jnp.inf)
        p = jax.nn.softmax(logits, axis=-1).astype(v.dtype)
        outs.append(jnp.einsum('bhqk,bkhd->bqhd', p, v[:, :end]))
    o = jnp.concatenate(outs, axis=1).reshape(Bsz, T, D)
    return o @ w_out


def conv_ffn(x, w_up, conv_w, conv_b, w_down):
    u = causal_dwconv(x @ w_up, conv_w, conv_b)
    gate, val = jnp.split(u, 2, axis=-1)
    return (jax.nn.silu(gate) * val) @ w_down


def setup_inputs(seed: int = 0) -> dict:
    key = jax.random.key(seed)
    ks = jax.random.split(key, 32)
    f32 = jnp.float32
    nrm = lambda k, shape, s: jax.random.normal(k, shape, f32) * s
    D, RW, CW = D_MODEL, RWKV_WIDTH, CONV_WIDTH
    ratio = jnp.linspace(0.0, 1.0, RW, dtype=f32)
    return {
        'x': nrm(ks[0], (BATCH, SEQ, D), 1.0),
        'ln_g': 1.0 + nrm(ks[1], (DEPTH, 2, D), 0.02),
        'ln_b': nrm(ks[2], (DEPTH, 2, D), 0.02),
        'ev_w_in': nrm(ks[3], (N_EVEN, D, EVEN_COLS), D ** -0.5),
        'ev_mu': jax.random.uniform(ks[4], (N_EVEN, RWKV_COLS), f32),
        'ev_w0': -6.5 + 5.0 * ratio ** 0.85 + nrm(ks[5], (N_EVEN, RW), 0.1),
        'ev_w2': nrm(ks[6], (N_EVEN, DECAY_LORA, RW), 0.1 * DECAY_LORA ** -0.5),
        'ev_a0': nrm(ks[7], (N_EVEN, RW), 0.1),
        'ev_a2': nrm(ks[8], (N_EVEN, AAA_LORA, RW), 0.5 * AAA_LORA ** -0.5),
        'ev_g2': nrm(ks[9], (N_EVEN, GATE_LORA, RW), GATE_LORA ** -0.5),
        'ev_k_k': 0.85 + nrm(ks[10], (N_EVEN, RW), 0.05),
        'ev_k_a': 1.0 + nrm(ks[11], (N_EVEN, RW), 0.05),
        'ev_r_k': -0.04 + nrm(ks[12], (N_EVEN, RWKV_HEADS, RWKV_HEAD), 0.05),
        'ev_lnx_g': 1.0 + nrm(ks[13], (N_EVEN, RW), 0.02),
        'ev_lnx_b': nrm(ks[14], (N_EVEN, RW), 0.02),
        'ev_v0': 1.0 + nrm(ks[15], (N_EVEN - 1, RW), 0.1),
        'ev_v1': nrm(ks[16], (N_EVEN - 1, RW, MV_LORA), RW ** -0.5),
        'ev_v2': nrm(ks[17], (N_EVEN - 1, MV_LORA, RW), MV_LORA ** -0.5),
        'ev_conv_w': nrm(ks[18], (N_EVEN, SHORT_CONV, CW), SHORT_CONV ** -0.5),
        'ev_w_out': nrm(ks[19], (N_EVEN, D, D), DN_BETA * D ** -0.5),
        'od_w_in': nrm(ks[20], (N_ODD, D, ODD_COLS), D ** -0.5),
        'od_b_f': jnp.linspace(1.0, 6.0, FOX_HEADS, dtype=f32) + nrm(ks[21], (N_ODD, FOX_HEADS), 0.1),
        'od_w_out': nrm(ks[22], (N_ODD, D, D), DN_BETA * D ** -0.5),
        'ff_w_up': nrm(ks[23], (DEPTH, D, 2 * D_FF), D ** -0.5),
        'ff_conv_w': nrm(ks[24], (DEPTH, FFN_CONV, 2 * D_FF), FFN_CONV ** -0.5),
        'ff_conv_b': nrm(ks[25], (DEPTH, 2 * D_FF), 0.02),
        'ff_w_down': nrm(ks[26], (DEPTH, D_FF, D), DN_BETA * D_FF ** -0.5),
    }


def reference(x, ln_g, ln_b, ev_w_in, ev_mu, ev_w0, ev_w2, ev_a0, ev_a2, ev_g2, ev_k_k, ev_k_a,
              ev_r_k, ev_lnx_g, ev_lnx_b, ev_v0, ev_v1, ev_v2, ev_conv_w, ev_w_out,
              od_w_in, od_b_f, od_w_out, ff_w_up, ff_conv_w, ff_conv_b, ff_w_down):
    v_first = None
    for i in range(DEPTH):
        if i % 2 == 0:
            e = i // 2
            v_res = (ev_v0[e - 1], ev_v1[e - 1], ev_v2[e - 1]) if e > 0 else None
            mix, v_first = rwkv_conv_mixer(x, ev_w_in[e], ev_mu[e], ev_w0[e], ev_w2[e], ev_a0[e],
                                           ev_a2[e], ev_g2[e], ev_k_k[e], ev_k_a[e], ev_r_k[e],
                                           ev_lnx_g[e], ev_lnx_b[e], ev_conv_w[e], ev_w_out[e],
                                           v_first, v_res)
        else:
            o = i // 2
            mix = fox_mixer(x, od_w_in[o], od_b_f[o], od_w_out[o])
        x = layer_norm(DN_ALPHA * x + mix, ln_g[i, 0], ln_b[i, 0])
        ffn = conv_ffn(x, ff_w_up[i], ff_conv_w[i], ff_conv_b[i], ff_w_down[i])
        x = layer_norm(DN_ALPHA * x + ffn, ln_g[i, 1], ln_b[i, 1])
    return x
```

```python
import functools

import jax
import jax.numpy as jnp
from jax import lax
from jax.experimental import pallas as pl
from jax.experimental.pallas import tpu as pltpu

F32 = jnp.float32
BF16 = jnp.bfloat16

D_MODEL = 2048
SEQ = 4096
DEPTH = 4
RW = 1024
HEAD = 64
DECAY_LORA = 64
AAA_LORA = 64
MV_LORA = 32
GATE_LORA = 160
CW = 1024
FOX_HEADS = 16
FOX_HEAD = 128
D_FF = 5632
GN_EPS = 64e-5
LN_EPS = 1e-5
DN_ALPHA = (2 * DEPTH) ** 0.25

LANES = 128
SUBLANES = 8
VMEM_LIMIT = 56 * 1024 * 1024

LORA_W = 3 * RW
LORA_A = LORA_W + LANES
LORA_G = LORA_A + LANES
RWKV_PAD = LORA_G + 2 * LANES

CHUNK = 64
GROUP = 4 * HEAD

NN = (((1,), (0,)), ((), ()))
NT = (((1,), (1,)), ((), ()))
TN = (((0,), (0,)), ((), ()))


def _cparams(*sem):
    return pltpu.CompilerParams(dimension_semantics=sem, vmem_limit_bytes=VMEM_LIMIT)


def _bdot(a, b, dims=NN):
    return lax.dot_general(a.astype(BF16), b.astype(BF16), dims, preferred_element_type=F32)


def _split3(x):
    hi = x.astype(BF16)
    r1 = x - hi.astype(F32)
    mid = r1.astype(BF16)
    lo = (r1 - mid.astype(F32)).astype(BF16)
    return hi, mid, lo


def _shift_rows(u, prev8, s):
    ext = jnp.concatenate([prev8, u], axis=0)
    return pltpu.roll(ext, s, axis=0)[SUBLANES:]


def _head_sum(x, ones_pair):
    outs = []
    for t in range(x.shape[1] // LANES):
        xt = x[:, t * LANES:(t + 1) * LANES]
        hi = xt.astype(BF16)
        lo = (xt - hi.astype(F32)).astype(BF16)
        outs.append(jnp.dot(jnp.concatenate([hi, lo], axis=1), ones_pair,
                            preferred_element_type=F32))
    return jnp.concatenate(outs, axis=1)


def _ones_pair():
    r = lax.broadcasted_iota(jnp.int32, (2 * LANES, LANES), 0)
    c = lax.broadcasted_iota(jnp.int32, (2 * LANES, LANES), 1)
    return jnp.where((r % LANES) // HEAD == c // HEAD, 1.0, 0.0).astype(BF16)


def _mm_kernel(x_ref, w_ref, o_ref):
    o_ref[...] = jnp.dot(x_ref[...], w_ref[...], preferred_element_type=F32).astype(o_ref.dtype)


def _mm(x, w, out_dtype, bm, bn):
    M, K = x.shape
    N = w.shape[1]
    return pl.pallas_call(
        _mm_kernel,
        grid=(N // bn, M // bm),
        in_specs=[pl.BlockSpec((bm, K), lambda n, m: (m, 0)),
                  pl.BlockSpec((K, bn), lambda n, m: (0, n))],
        out_specs=pl.BlockSpec((bm, bn), lambda n, m: (m, n)),
        out_shape=jax.ShapeDtypeStruct((M, N), out_dtype),
        compiler_params=_cparams("parallel", "parallel"),
        name="mm",
    )(x, w)


def _ln_res_kernel(x_ref, y_ref, g_ref, b_ref, o_ref, ob_ref):
    z = DN_ALPHA * x_ref[...] + y_ref[...]
    mu = jnp.mean(z, axis=-1, keepdims=True)
    d = z - mu
    var = jnp.mean(d * d, axis=-1, keepdims=True)
    o = d * lax.rsqrt(var + LN_EPS) * g_ref[...] + b_ref[...]
    o_ref[...] = o
    ob_ref[...] = o.astype(BF16)


def _ln_res(x, y, g, b, bm=512):
    M, D = x.shape
    row = pl.BlockSpec((bm, D), lambda m: (m, 0))
    vec = pl.BlockSpec((1, D), lambda m: (0, 0))
    return pl.pallas_call(
        _ln_res_kernel,
        grid=(M // bm,),
        in_specs=[row, row, vec, vec],
        out_specs=[row, row],
        out_shape=[jax.ShapeDtypeStruct((M, D), F32), jax.ShapeDtypeStruct((M, D), BF16)],
        compiler_params=_cparams("parallel"),
        name="ln_res",
    )(x, y, g.reshape(1, D), b.reshape(1, D))


def _ffn_up_kernel(x_ref, wg_ref, wv_ref, cwg_ref, cwv_ref, cbg_ref, cbv_ref, h_ref,
                   carry_g, carry_v, *, blocks_per_seq):
    @pl.when(pl.program_id(1) % blocks_per_seq == 0)
    def _():
        carry_g[...] = jnp.zeros_like(carry_g)
        carry_v[...] = jnp.zeros_like(carry_v)

    x = x_ref[...]

    def branch(w_ref, cw_ref, cb_ref, carry):
        u = jnp.dot(x, w_ref[...], preferred_element_type=F32)
        prev = carry[...]
        cw = cw_ref[...]
        y = (_shift_rows(u, prev, 2) * cw[0:1] + _shift_rows(u, prev, 1) * cw[1:2]
             + u * cw[2:3] + cb_ref[...])
        carry[...] = u[u.shape[0] - SUBLANES:]
        return y

    gate = branch(wg_ref, cwg_ref, cbg_ref, carry_g)
    val = branch(wv_ref, cwv_ref, cbv_ref, carry_v)
    h_ref[...] = (gate * jax.nn.sigmoid(gate) * val).astype(h_ref.dtype)


def _ffn_up(x, w_up, conv_w, conv_b, bm=512, bn=512):
    M, K = x.shape
    nb = D_FF // bn
    kern = functools.partial(_ffn_up_kernel, blocks_per_seq=SEQ // bm)
    return pl.pallas_call(
        kern,
        grid=(nb, M // bm),
        in_specs=[pl.BlockSpec((bm, K), lambda n, m: (m, 0)),
                  pl.BlockSpec((K, bn), lambda n, m: (0, n)),
                  pl.BlockSpec((K, bn), lambda n, m: (0, n + nb)),
                  pl.BlockSpec((3, bn), lambda n, m: (0, n)),
                  pl.BlockSpec((3, bn), lambda n, m: (0, n + nb)),
                  pl.BlockSpec((1, bn), lambda n, m: (0, n)),
                  pl.BlockSpec((1, bn), lambda n, m: (0, n + nb))],
        out_specs=pl.BlockSpec((bm, bn), lambda n, m: (m, n)),
        out_shape=jax.ShapeDtypeStruct((M, D_FF), BF16),
        scratch_shapes=[pltpu.VMEM((SUBLANES, bn), F32), pltpu.VMEM((SUBLANES, bn), F32)],
        compiler_params=_cparams("parallel", "arbitrary"),
        name="ffn_up",
    )(x, w_up, w_up, conv_w, conv_w, conv_b, conv_b)


def _sconv_kernel(x_ref, wb_ref, wc_ref, wh_ref, cw_ref, o_ref, carry, *, blocks_per_seq):
    @pl.when(pl.program_id(1) % blocks_per_seq == 0)
    def _():
        carry[...] = jnp.zeros_like(carry)

    x = x_ref[...]
    gb = jnp.dot(x, wb_ref[...], preferred_element_type=F32)
    gc = jnp.dot(x, wc_ref[...], preferred_element_type=F32)
    h = jnp.dot(x, wh_ref[...], preferred_element_type=F32)
    u = gc * h
    prev = carry[...]
    cw = cw_ref[...]
    y = _shift_rows(u, prev, 2) * cw[0:1] + _shift_rows(u, prev, 1) * cw[1:2] + u * cw[2:3]
    carry[...] = u[u.shape[0] - SUBLANES:]
    o_ref[...] = (gb * y).astype(o_ref.dtype)


def _sconv(x, w_conv, conv_w, bm=512, bn=512):
    M, K = x.shape
    nb = CW // bn
    kern = functools.partial(_sconv_kernel, blocks_per_seq=SEQ // bm)
    return pl.pallas_call(
        kern,
        grid=(nb, M // bm),
        in_specs=[pl.BlockSpec((bm, K), lambda n, m: (m, 0)),
                  pl.BlockSpec((K, bn), lambda n, m: (0, n)),
                  pl.BlockSpec((K, bn), lambda n, m: (0, n + nb)),
                  pl.BlockSpec((K, bn), lambda n, m: (0, n + 2 * nb)),
                  pl.BlockSpec((3, bn), lambda n, m: (0, n))],
        out_specs=pl.BlockSpec((bm, bn), lambda n, m: (m, n)),
        out_shape=jax.ShapeDtypeStruct((M, CW), BF16),
        scratch_shapes=[pltpu.VMEM((SUBLANES, bn), F32)],
        compiler_params=_cparams("parallel", "arbitrary"),
        name="sconv",
    )(x, w_conv, w_conv, w_conv, conv_w)


def _rwkv_prep_kernel(*refs, has_vres, blocks_per_seq):
    if has_vres:
        (pr_ref, prev_ref, mu_ref, w0_ref, w2_ref, a0_ref, a2_ref, g2_ref, kk_ref, ka_ref,
         v0_ref, v1_ref, v2_ref, vf_ref,
         r_o, lw_o, k_o, v_o, kkn_o, lr_o, g_o) = refs
    else:
        (pr_ref, prev_ref, mu_ref, w0_ref, w2_ref, a0_ref, a2_ref, g2_ref, kk_ref, ka_ref,
         r_o, lw_o, k_o, v_o, kkn_o, lr_o, g_o) = refs
    first = pl.program_id(0) % blocks_per_seq == 0
    pr = pr_ref[...]
    prev = jnp.where(first, 0.0, prev_ref[...])
    x = pr + (_shift_rows(pr, prev, 1) - pr) * mu_ref[...]
    r = x[:, 0:RW]
    k = x[:, RW:2 * RW]
    v = x[:, 2 * RW:3 * RW]
    dw = x[:, LORA_W:LORA_W + LANES]
    da = x[:, LORA_A:LORA_A + LANES]
    dg = x[:, LORA_G:LORA_G + 2 * LANES]
    z = w0_ref[...] + _bdot(jnp.tanh(dw), w2_ref[...])
    lw = -jnp.exp(F32(-0.5)) * jax.nn.sigmoid(z)
    lr = jax.nn.sigmoid(a0_ref[...] + _bdot(da, a2_ref[...]))
    g = _bdot(jax.nn.sigmoid(dg), g2_ref[...])
    if has_vres:
        mix = jax.nn.sigmoid(v0_ref[...] + _bdot(_bdot(v, v1_ref[...]), v2_ref[...]))
        v = v + (vf_ref[...] - v) * mix
    kkr = k * kk_ref[...]
    ss = _head_sum(kkr * kkr, _ones_pair())
    kkn = kkr / jnp.maximum(jnp.sqrt(ss), 1e-12)
    k2 = k * (1.0 + (lr - 1.0) * ka_ref[...])
    r_o[...] = r
    lw_o[...] = lw
    k_o[...] = k2
    v_o[...] = v
    kkn_o[...] = kkn
    lr_o[...] = lr
    g_o[...] = g


def _rwkv_prep(pr, mu, w0, w2, a0, a2, g2, k_k, k_a, vres, bm=256):
    M = pr.shape[0]
    has_vres = vres is not None
    row = lambda w: pl.BlockSpec((bm, w), lambda m: (m, 0))
    full = lambda a: pl.BlockSpec(a.shape, lambda m: (0, 0))
    per8 = bm // SUBLANES
    args = [pr, pr, mu, w0, w2, a0, a2, g2, k_k, k_a]
    in_specs = [row(RWKV_PAD),
                pl.BlockSpec((SUBLANES, RWKV_PAD), lambda m: (jnp.maximum(m * per8 - 1, 0), 0)),
                full(mu), full(w0), full(w2), full(a0), full(a2), full(g2), full(k_k), full(k_a)]
    if has_vres:
        v0, v1, v2, vf = vres
        args += [v0, v1, v2, vf]
        in_specs += [full(v0), full(v1), full(v2), row(RW)]
    kern = functools.partial(_rwkv_prep_kernel, has_vres=has_vres, blocks_per_seq=SEQ // bm)
    return pl.pallas_call(
        kern,
        grid=(M // bm,),
        in_specs=in_specs,
        out_specs=[row(RW)] * 7,
        out_shape=[jax.ShapeDtypeStruct((M, RW), F32)] * 7,
        compiler_params=_cparams("parallel"),
        name="rwkv_prep",
    )(*args)


def _scan_kernel(r_ref, lw_ref, k_ref, v_ref, kk_ref, lr_ref, g_ref, rk_ref, lg_ref, lb_ref,
                 o_ref, h_sc, *, chunks_per_seq):
    C = CHUNK

    @pl.when(pl.program_id(0) % chunks_per_seq == 0)
    def _():
        h_sc[...] = jnp.zeros_like(h_sc)

    row = lax.broadcasted_iota(jnp.int32, (C, GROUP), 0)
    col = lax.broadcasted_iota(jnp.int32, (C, GROUP), 1) % HEAD
    low_strict = (row > col).astype(F32)
    low_incl = (row >= col).astype(F32)
    eye_sbs = (row == col).astype(F32)
    br = lax.broadcasted_iota(jnp.int32, (GROUP, GROUP), 0)
    bc = lax.broadcasted_iota(jnp.int32, (GROUP, GROUP), 1)
    bd_mask = (br // HEAD == bc // HEAD).astype(F32)
    eye_bd = (br == bc).astype(F32)

    def bd(x):
        return (jnp.concatenate([x, x, x, x], axis=0) * bd_mask).astype(BF16)

    tr = lax.broadcasted_iota(jnp.int32, (C, 3 * C), 0)
    tc = lax.broadcasted_iota(jnp.int32, (C, 3 * C), 1) % C
    tri3 = (tr >= tc).astype(BF16)
    lw_all = lw_ref[...]
    cum_all = jnp.dot(tri3, jnp.concatenate(_split3(lw_all), axis=0), preferred_element_type=F32)

    ys = []
    for gi in range(RW // GROUP):
        sl = slice(gi * GROUP, (gi + 1) * GROUP)
        lw = lw_all[:, sl]
        cum = cum_all[:, sl]
        tot = cum[C - 1:C]
        r = r_ref[:, sl]
        k = k_ref[:, sl]
        v = v_ref[:, sl]
        kk = kk_ref[:, sl]
        b = kk * lr_ref[:, sl]
        e_neg = jnp.exp(-cum)
        e_end = jnp.exp(tot - cum)
        a_t = -kk * jnp.exp(cum - lw)
        r_t = r * jnp.exp(cum)
        b_t = b * e_neg
        k_t = k * e_neg
        b_e = b * e_end
        k_e = k * e_end

        ar = jnp.concatenate([a_t, r_t], axis=0)
        p_b = _bdot(ar, bd(b_t), NT)
        p_k = _bdot(ar, bd(k_t), NT)
        l_ab = p_b[:C] * low_strict
        m_rb = p_b[C:] * low_incl
        l_ak = p_k[:C] * low_strict
        m_rk = p_k[C:] * low_incl

        t_m = eye_sbs + l_ab
        p = _bdot(l_ab, bd(l_ab))
        for _ in range(4):
            both = _bdot(jnp.concatenate([p, t_m], axis=0), bd(p))
            p = both[:C]
            t_m = t_m + both[C:]
        t_m = t_m + _bdot(t_m, bd(p))

        bd_v = bd(v)
        lv = _bdot(l_ak, bd_v)
        a_h = _bdot(t_m, bd(a_t))
        u0 = _bdot(t_m, bd(lv))
        r_h = r_t + _bdot(m_rb, bd(a_h))
        y0 = _bdot(m_rb, bd(u0)) + _bdot(m_rk, bd_v)
        phi = (eye_bd * jnp.exp(tot) + _bdot(b_e, a_h, TN)) * bd_mask
        psi = (_bdot(b_e, u0, TN) + _bdot(k_e, v, TN)) * bd_mask
        h0 = h_sc[gi]
        ys.append(_bdot(r_h, h0) + y0)
        h_sc[gi] = _bdot(phi, h0) + psi

    y = jnp.concatenate(ys, axis=1)
    ones_pair = _ones_pair()
    m = _head_sum(y, ones_pair) * (1.0 / HEAD)
    d = y - m
    var = _head_sum(d * d, ones_pair) * (1.0 / HEAD)
    yn = d * lax.rsqrt(var + GN_EPS) * lg_ref[...] + lb_ref[...]
    bonus = _head_sum(r_ref[...] * k_ref[...] * rk_ref[...], ones_pair) * v_ref[...]
    o_ref[...] = ((yn + bonus) * g_ref[...]).astype(o_ref.dtype)


def _scan(r, lw, k, v, kkn, lr, g, r_k, lnx_g, lnx_b):
    M = r.shape[0]
    row = pl.BlockSpec((CHUNK, RW), lambda i: (i, 0))
    vec = pl.BlockSpec((1, RW), lambda i: (0, 0))
    kern = functools.partial(_scan_kernel, chunks_per_seq=SEQ // CHUNK)
    return pl.pallas_call(
        kern,
        grid=(M // CHUNK,),
        in_specs=[row] * 7 + [vec] * 3,
        out_specs=row,
        out_shape=jax.ShapeDtypeStruct((M, RW), BF16),
        scratch_shapes=[pltpu.VMEM((RW // GROUP, GROUP, GROUP), F32)],
        compiler_params=_cparams("arbitrary"),
        name="rwkv_scan",
    )(r, lw, k, v, kkn, lr, g, r_k, lnx_g, lnx_b)


def _fox_gate_kernel(fl_ref, bf_ref, c_ref, carry, *, blocks_per_seq):
    @pl.when(pl.program_id(0) % blocks_per_seq == 0)
    def _():
        carry[...] = jnp.zeros_like(carry)

    bm = fl_ref.shape[0]
    lf = jax.nn.log_sigmoid(fl_ref[...] + bf_ref[...])
    tr = lax.broadcasted_iota(jnp.int32, (bm, 3 * bm), 0)
    tc = lax.broadcasted_iota(jnp.int32, (bm, 3 * bm), 1) % bm
    tri3 = (tr >= tc).astype(BF16)
    c = jnp.dot(tri3, jnp.concatenate(_split3(lf), axis=0), preferred_element_type=F32) + carry[...]
    c_ref[...] = c
    carry[...] = c[bm - 1:bm]


def _fox_gate(fl, b_f, bm=512):
    M = fl.shape[0]
    kern = functools.partial(_fox_gate_kernel, blocks_per_seq=SEQ // bm)
    return pl.pallas_call(
        kern,
        grid=(M // bm,),
        in_specs=[pl.BlockSpec((bm, LANES), lambda m: (m, 0)),
                  pl.BlockSpec((1, LANES), lambda m: (0, 0))],
        out_specs=pl.BlockSpec((bm, LANES), lambda m: (m, 0)),
        out_shape=jax.ShapeDtypeStruct((M, LANES), F32),
        scratch_shapes=[pltpu.VMEM((1, LANES), F32)],
        compiler_params=_cparams("arbitrary"),
        name="fox_gate",
    )(fl, b_f)


def _flash_kernel(q_ref, k_ref, v_ref, cq_ref, ck_ref, o_ref, m_sc, l_sc, acc_sc, *, blk, scale):
    qi = pl.program_id(2)
    ki = pl.program_id(3)

    @pl.when(ki == 0)
    def _():
        m_sc[...] = jnp.full_like(m_sc, -jnp.inf)
        l_sc[...] = jnp.zeros_like(l_sc)
        acc_sc[...] = jnp.zeros_like(acc_sc)

    @pl.when(ki <= qi)
    def _():
        s = lax.dot_general(q_ref[0], k_ref[0], NT, preferred_element_type=F32) * scale
        s = s + (cq_ref[0, 0] - ck_ref[0, 0])
        row = qi * blk + lax.broadcasted_iota(jnp.int32, (blk, blk), 0)
        col = ki * blk + lax.broadcasted_iota(jnp.int32, (blk, blk), 1)
        s = jnp.where(row >= col, s, -jnp.inf)
        m_old = m_sc[...]
        m_new = jnp.maximum(m_old, jnp.max(s, axis=-1, keepdims=True))
        alpha = jnp.exp(m_old - m_new)
        p = jnp.exp(s - m_new)
        l_sc[...] = alpha * l_sc[...] + jnp.sum(p, axis=-1, keepdims=True)
        acc_sc[...] = alpha * acc_sc[...] + jnp.dot(p.astype(BF16), v_ref[0],
                                                    preferred_element_type=F32)
        m_sc[...] = m_new

    @pl.when(ki == qi)
    def _():
        o_ref[0] = (acc_sc[...] / l_sc[...]).astype(o_ref.dtype)


def _flash(qkv, c_col, c_row, blk=512):
    B, T, _ = qkv.shape
    H = FOX_HEADS
    kern = functools.partial(_flash_kernel, blk=blk, scale=FOX_HEAD ** -0.5)
    return pl.pallas_call(
        kern,
        grid=(B, H, T // blk, T // blk),
        in_specs=[pl.BlockSpec((1, blk, FOX_HEAD), lambda b, h, qi, ki: (b, qi, h)),
                  pl.BlockSpec((1, blk, FOX_HEAD), lambda b, h, qi, ki: (b, jnp.minimum(ki, qi), H + h)),
                  pl.BlockSpec((1, blk, FOX_HEAD), lambda b, h, qi, ki: (b, jnp.minimum(ki, qi), 2 * H + h)),
                  pl.BlockSpec((1, 1, blk, 1), lambda b, h, qi, ki: (b, h, qi, 0)),
                  pl.BlockSpec((1, 1, 1, blk), lambda b, h, qi, ki: (b, h, 0, jnp.minimum(ki, qi)))],
        out_specs=pl.BlockSpec((1, blk, FOX_HEAD), lambda b, h, qi, ki: (b, qi, h)),
        out_shape=jax.ShapeDtypeStruct((B, T, D_MODEL), BF16),
        scratch_shapes=[pltpu.VMEM((blk, 1), F32), pltpu.VMEM((blk, 1), F32),
                        pltpu.VMEM((blk, FOX_HEAD), F32)],
        compiler_params=_cparams("parallel", "parallel", "parallel", "arbitrary"),
        name="fox_flash",
    )(qkv, qkv, qkv, c_col, c_row)


def _pad_cols(w, width):
    return jnp.pad(w, ((0, 0), (0, width - w.shape[1])))


def _pad_rows(w, height):
    return jnp.pad(w, ((0, height - w.shape[0]), (0, 0)))


def _even_layer(xb, w_in, mu, w0, w2, a0, a2, g2, k_k, k_a, r_k, lnx_g, lnx_b,
                conv_w, w_out, v_first, vres):
    rw3 = 3 * RW
    o_dw, o_da, o_dg = rw3, rw3 + DECAY_LORA, rw3 + DECAY_LORA + AAA_LORA
    n_rwkv = o_dg + GATE_LORA

    def lay(a):
        return jnp.concatenate([a[..., :rw3],
                                _pad_cols(a[..., o_dw:o_da], LANES),
                                _pad_cols(a[..., o_da:o_dg], LANES),
                                _pad_cols(a[..., o_dg:n_rwkv], 2 * LANES)], axis=-1)

    w_r = lay(w_in[:, :n_rwkv]).astype(BF16)
    w_c = w_in[:, n_rwkv:].astype(BF16)
    pr = _mm(xb, w_r, F32, bm=512, bn=RWKV_PAD // 7)
    y_conv = _sconv(xb, w_c, conv_w)
    vec = lambda a: a.reshape(1, -1)
    if vres is not None:
        v0, v1, v2 = vres
        vres_args = (vec(v0), _pad_cols(v1, LANES).astype(BF16), _pad_rows(v2, LANES).astype(BF16), v_first)
    else:
        vres_args = None
    r, lw, k, v, kkn, lr, g = _rwkv_prep(
        pr, lay(vec(mu)), vec(w0), _pad_rows(w2, LANES).astype(BF16), vec(a0),
        _pad_rows(a2, LANES).astype(BF16), _pad_rows(g2, 2 * LANES).astype(BF16),
        vec(k_k), vec(k_a), vres_args)
    if vres is None:
        v_first = v
    y_rwkv = _scan(r, lw, k, v, kkn, lr, g, vec(r_k), vec(lnx_g), vec(lnx_b))
    cat = jnp.concatenate([y_rwkv, y_conv], axis=-1)
    return _mm(cat, w_out.astype(BF16), F32, bm=1024, bn=512), v_first


def _odd_layer(xb, w_in, b_f, w_out):
    M = xb.shape[0]
    B = M // SEQ
    d3 = 3 * D_MODEL
    qkv = _mm(xb, w_in[:, :d3].astype(BF16), BF16, bm=1024, bn=512)
    fl = _mm(xb, _pad_cols(w_in[:, d3:], LANES).astype(BF16), F32, bm=1024, bn=LANES)
    c = _fox_gate(fl, _pad_cols(b_f.reshape(1, -1), LANES))
    c = c.reshape(B, SEQ, LANES)[:, :, :FOX_HEADS]
    c_t = jnp.swapaxes(c, 1, 2)
    o = _flash(qkv.reshape(B, SEQ, d3), c_t[..., None], c_t[:, :, None, :])
    return _mm(o.reshape(M, D_MODEL), w_out.astype(BF16), F32, bm=1024, bn=512)


def _conv_ffn(xb, w_up, conv_w, conv_b, w_down):
    h = _ffn_up(xb, w_up.astype(BF16), conv_w, conv_b.reshape(1, -1))
    return _mm(h, w_down.astype(BF16), F32, bm=512, bn=512)


def kernel(x, ln_g, ln_b, ev_w_in, ev_mu, ev_w0, ev_w2, ev_a0, ev_a2, ev_g2, ev_k_k, ev_k_a, ev_r_k, ev_lnx_g, ev_lnx_b, ev_v0, ev_v1, ev_v2, ev_conv_w, ev_w_out, od_w_in, od_b_f, od_w_out, ff_w_up, ff_conv_w, ff_conv_b, ff_w_down):
    B, T, D = x.shape
    xf = x.reshape(B * T, D)
    xb = xf.astype(BF16)
    v_first = None
    for i in range(DEPTH):
        if i % 2 == 0:
            e = i // 2
            vres = (ev_v0[e - 1], ev_v1[e - 1], ev_v2[e - 1]) if e > 0 else None
            mix, v_first = _even_layer(xb, ev_w_in[e], ev_mu[e], ev_w0[e], ev_w2[e], ev_a0[e],
                                       ev_a2[e], ev_g2[e], ev_k_k[e], ev_k_a[e], ev_r_k[e],
                                       ev_lnx_g[e], ev_lnx_b[e], ev_conv_w[e], ev_w_out[e],
                                       v_first, vres)
        else:
            o = i // 2
            mix = _odd_layer(xb, od_w_in[o], od_b_f[o], od_w_out[o])
        xf, xb = _ln_res(xf, mix, ln_g[i, 0], ln_b[i, 0])
        ffn = _conv_ffn(xb, ff_w_up[i], ff_conv_w[i], ff_conv_b[i], ff_w_down[i])
        xf, xb = _ln_res(xf, ffn, ln_g[i, 1], ln_b[i, 1])
    return xf.reshape(B, T, D)
```

```python
import functools

import jax
import jax.numpy as jnp
from jax import lax
from jax.experimental import pallas as pl
from jax.experimental.pallas import tpu as pltpu

F32 = jnp.float32
BF16 = jnp.bfloat16

D_MODEL = 2048
SEQ = 4096
DEPTH = 4
RW = 1024
HEAD = 64
DECAY_LORA = 64
AAA_LORA = 64
MV_LORA = 32
GATE_LORA = 160
CW = 1024
FOX_HEADS = 16
FOX_HEAD = 128
D_FF = 5632
GN_EPS = 64e-5
LN_EPS = 1e-5
DN_ALPHA = (2 * DEPTH) ** 0.25

LANES = 128
SUBLANES = 8
VMEM_LIMIT = 56 * 1024 * 1024

LORA_W = 3 * RW
LORA_A = LORA_W + LANES
LORA_G = LORA_A + LANES
RWKV_PAD = LORA_G + 2 * LANES

CHUNK = 64
GROUP = 4 * HEAD

NN = (((1,), (0,)), ((), ()))
NT = (((1,), (1,)), ((), ()))
TN = (((0,), (0,)), ((), ()))


def _cparams(*sem):
    return pltpu.CompilerParams(dimension_semantics=sem, vmem_limit_bytes=VMEM_LIMIT)


def _bdot(a, b, dims=NN):
    return lax.dot_general(a.astype(BF16), b.astype(BF16), dims, preferred_element_type=F32)


def _split3(x):
    hi = x.astype(BF16)
    r1 = x - hi.astype(F32)
    mid = r1.astype(BF16)
    lo = (r1 - mid.astype(F32)).astype(BF16)
    return hi, mid, lo


def _shift_rows(u, prev8, s):
    ext = jnp.concatenate([prev8, u], axis=0)
    return pltpu.roll(ext, s, axis=0)[SUBLANES:]


def _head_sum(x, ones_pair):
    outs = []
    for t in range(x.shape[1] // LANES):
        xt = x[:, t * LANES:(t + 1) * LANES]
        hi = xt.astype(BF16)
        lo = (xt - hi.astype(F32)).astype(BF16)
        outs.append(jnp.dot(jnp.concatenate([hi, lo], axis=1), ones_pair,
                            preferred_element_type=F32))
    return jnp.concatenate(outs, axis=1)


def _ones_pair():
    r = lax.broadcasted_iota(jnp.int32, (2 * LANES, LANES), 0)
    c = lax.broadcasted_iota(jnp.int32, (2 * LANES, LANES), 1)
    return jnp.where((r % LANES) // HEAD == c // HEAD, 1.0, 0.0).astype(BF16)


def _mm_kernel(x_ref, w_ref, o_ref):
    o_ref[...] = jnp.dot(x_ref[...], w_ref[...], preferred_element_type=F32).astype(o_ref.dtype)


def _mm(x, w, out_dtype, bm, bn):
    M, K = x.shape
    N = w.shape[1]
    return pl.pallas_call(
        _mm_kernel,
        grid=(N // bn, M // bm),
        in_specs=[pl.BlockSpec((bm, K), lambda n, m: (m, 0)),
                  pl.BlockSpec((K, bn), lambda n, m: (0, n))],
        out_specs=pl.BlockSpec((bm, bn), lambda n, m: (m, n)),
        out_shape=jax.ShapeDtypeStruct((M, N), out_dtype),
        compiler_params=_cparams("parallel", "parallel"),
        name="mm",
    )(x, w)


def _ln_res_kernel(x_ref, y_ref, g_ref, b_ref, o_ref, ob_ref):
    z = DN_ALPHA * x_ref[...] + y_ref[...]
    mu = jnp.mean(z, axis=-1, keepdims=True)
    d = z - mu
    var = jnp.mean(d * d, axis=-1, keepdims=True)
    o = d * lax.rsqrt(var + LN_EPS) * g_ref[...] + b_ref[...]
    o_ref[...] = o
    ob_ref[...] = o.astype(BF16)


def _ln_res(x, y, g, b, bm=512):
    M, D = x.shape
    row = pl.BlockSpec((bm, D), lambda m: (m, 0))
    vec = pl.BlockSpec((1, D), lambda m: (0, 0))
    return pl.pallas_call(
        _ln_res_kernel,
        grid=(M // bm,),
        in_specs=[row, row, vec, vec],
        out_specs=[row, row],
        out_shape=[jax.ShapeDtypeStruct((M, D), F32), jax.ShapeDtypeStruct((M, D), BF16)],
        compiler_params=_cparams("parallel"),
        name="ln_res",
    )(x, y, g.reshape(1, D), b.reshape(1, D))


def _ffn_up_kernel(x_ref, wg_ref, wv_ref, cwg_ref, cwv_ref, cbg_ref, cbv_ref, h_ref,
                   carry_g, carry_v, *, blocks_per_seq):
    @pl.when(pl.program_id(1) % blocks_per_seq == 0)
    def _():
        carry_g[...] = jnp.zeros_like(carry_g)
        carry_v[...] = jnp.zeros_like(carry_v)

    x = x_ref[...]
    bm = x.shape[0]
    prev_g = carry_g[...]
    prev_v = carry_v[...]
    sub = 2 * LANES

    def branch(w_ref, cw_ref, cb_ref, prev, cs):
        u = jnp.dot(x, w_ref[:, cs], preferred_element_type=F32)
        cw = cw_ref[:, cs]
        y = (_shift_rows(u, prev[:, cs], 2) * cw[0:1] + _shift_rows(u, prev[:, cs], 1) * cw[1:2]
             + u * cw[2:3] + cb_ref[:, cs])
        return y, u[bm - SUBLANES:]

    tails_g, tails_v = [], []
    for c in range(h_ref.shape[1] // sub):
        cs = slice(c * sub, (c + 1) * sub)
        gate, tg = branch(wg_ref, cwg_ref, cbg_ref, prev_g, cs)
        val, tv = branch(wv_ref, cwv_ref, cbv_ref, prev_v, cs)
        h_ref[:, cs] = (gate * jax.nn.sigmoid(gate) * val).astype(h_ref.dtype)
        tails_g.append(tg)
        tails_v.append(tv)
    carry_g[...] = jnp.concatenate(tails_g, axis=1)
    carry_v[...] = jnp.concatenate(tails_v, axis=1)


def _ffn_up(x, w_up, conv_w, conv_b, bm=1024, bn=512):
    M, K = x.shape
    nb = D_FF // bn
    kern = functools.partial(_ffn_up_kernel, blocks_per_seq=SEQ // bm)
    return pl.pallas_call(
        kern,
        grid=(nb, M // bm),
        in_specs=[pl.BlockSpec((bm, K), lambda n, m: (m, 0)),
                  pl.BlockSpec((K, bn), lambda n, m: (0, n)),
                  pl.BlockSpec((K, bn), lambda n, m: (0, n + nb)),
                  pl.BlockSpec((3, bn), lambda n, m: (0, n)),
                  pl.BlockSpec((3, bn), lambda n, m: (0, n + nb)),
                  pl.BlockSpec((1, bn), lambda n, m: (0, n)),
                  pl.BlockSpec((1, bn), lambda n, m: (0, n + nb))],
        out_specs=pl.BlockSpec((bm, bn), lambda n, m: (m, n)),
        out_shape=jax.ShapeDtypeStruct((M, D_FF), BF16),
        scratch_shapes=[pltpu.VMEM((SUBLANES, bn), F32), pltpu.VMEM((SUBLANES, bn), F32)],
        compiler_params=_cparams("parallel", "arbitrary"),
        name="ffn_up",
    )(x, w_up, w_up, conv_w, conv_w, conv_b, conv_b)


def _sconv_kernel(x_ref, wb_ref, wc_ref, wh_ref, cw_ref, o_ref, carry, *, blocks_per_seq):
    @pl.when(pl.program_id(1) % blocks_per_seq == 0)
    def _():
        carry[...] = jnp.zeros_like(carry)

    x = x_ref[...]
    gb = jnp.dot(x, wb_ref[...], preferred_element_type=F32)
    gc = jnp.dot(x, wc_ref[...], preferred_element_type=F32)
    h = jnp.dot(x, wh_ref[...], preferred_element_type=F32)
    u = gc * h
    prev = carry[...]
    cw = cw_ref[...]
    y = _shift_rows(u, prev, 2) * cw[0:1] + _shift_rows(u, prev, 1) * cw[1:2] + u * cw[2:3]
    carry[...] = u[u.shape[0] - SUBLANES:]
    o_ref[...] = (gb * y).astype(o_ref.dtype)


def _sconv(x, w_conv, conv_w, bm=512, bn=512):
    M, K = x.shape
    nb = CW // bn
    kern = functools.partial(_sconv_kernel, blocks_per_seq=SEQ // bm)
    return pl.pallas_call(
        kern,
        grid=(nb, M // bm),
        in_specs=[pl.BlockSpec((bm, K), lambda n, m: (m, 0)),
                  pl.BlockSpec((K, bn), lambda n, m: (0, n)),
                  pl.BlockSpec((K, bn), lambda n, m: (0, n + nb)),
                  pl.BlockSpec((K, bn), lambda n, m: (0, n + 2 * nb)),
                  pl.BlockSpec((3, bn), lambda n, m: (0, n))],
        out_specs=pl.BlockSpec((bm, bn), lambda n, m: (m, n)),
        out_shape=jax.ShapeDtypeStruct((M, CW), BF16),
        scratch_shapes=[pltpu.VMEM((SUBLANES, bn), F32)],
        compiler_params=_cparams("parallel", "arbitrary"),
        name="sconv",
    )(x, w_conv, w_conv, w_conv, conv_w)


def _rwkv_prep_kernel(*refs, has_vres, blocks_per_seq):
    if has_vres:
        (pr_ref, prev_ref, mu_ref, w0_ref, w2_ref, a0_ref, a2_ref, g2_ref, kk_ref, ka_ref,
         v0_ref, v1_ref, v2_ref, vf_ref,
         r_o, lw_o, k_o, v_o, kkn_o, lr_o, g_o) = refs
    else:
        (pr_ref, prev_ref, mu_ref, w0_ref, w2_ref, a0_ref, a2_ref, g2_ref, kk_ref, ka_ref,
         r_o, lw_o, k_o, v_o, kkn_o, lr_o, g_o) = refs
    first = pl.program_id(0) % blocks_per_seq == 0
    pr = pr_ref[...]
    prev = jnp.where(first, 0.0, prev_ref[...])
    x = pr + (_shift_rows(pr, prev, 1) - pr) * mu_ref[...]
    r = x[:, 0:RW]
    k = x[:, RW:2 * RW]
    v = x[:, 2 * RW:3 * RW]
    dw = x[:, LORA_W:LORA_W + LANES]
    da = x[:, LORA_A:LORA_A + LANES]
    dg = x[:, LORA_G:LORA_G + 2 * LANES]
    z = w0_ref[...] + _bdot(jnp.tanh(dw), w2_ref[...])
    lw = -jnp.exp(F32(-0.5)) * jax.nn.sigmoid(z)
    lr = jax.nn.sigmoid(a0_ref[...] + _bdot(da, a2_ref[...]))
    g = _bdot(jax.nn.sigmoid(dg), g2_ref[...])
    if has_vres:
        mix = jax.nn.sigmoid(v0_ref[...] + _bdot(_bdot(v, v1_ref[...]), v2_ref[...]))
        v = v + (vf_ref[...] - v) * mix
    kkr = k * kk_ref[...]
    ss = _head_sum(kkr * kkr, _ones_pair())
    kkn = kkr / jnp.maximum(jnp.sqrt(ss), 1e-12)
    k2 = k * (1.0 + (lr - 1.0) * ka_ref[...])
    r_o[...] = r
    lw_o[...] = lw
    k_o[...] = k2
    v_o[...] = v
    kkn_o[...] = kkn
    lr_o[...] = lr
    g_o[...] = g


def _rwkv_prep(pr, mu, w0, w2, a0, a2, g2, k_k, k_a, vres, bm=256):
    M = pr.shape[0]
    has_vres = vres is not None
    row = lambda w: pl.BlockSpec((bm, w), lambda m: (m, 0))
    full = lambda a: pl.BlockSpec(a.shape, lambda m: (0, 0))
    per8 = bm // SUBLANES
    args = [pr, pr, mu, w0, w2, a0, a2, g2, k_k, k_a]
    in_specs = [row(RWKV_PAD),
                pl.BlockSpec((SUBLANES, RWKV_PAD), lambda m: (jnp.maximum(m * per8 - 1, 0), 0)),
                full(mu), full(w0), full(w2), full(a0), full(a2), full(g2), full(k_k), full(k_a)]
    if has_vres:
        v0, v1, v2, vf = vres
        args += [v0, v1, v2, vf]
        in_specs += [full(v0), full(v1), full(v2), row(RW)]
    kern = functools.partial(_rwkv_prep_kernel, has_vres=has_vres, blocks_per_seq=SEQ // bm)
    return pl.pallas_call(
        kern,
        grid=(M // bm,),
        in_specs=in_specs,
        out_specs=[row(RW)] * 7,
        out_shape=[jax.ShapeDtypeStruct((M, RW), F32)] * 7,
        compiler_params=_cparams("parallel"),
        name="rwkv_prep",
    )(*args)


def _scan_kernel(r_ref, lw_ref, k_ref, v_ref, kk_ref, lr_ref, g_ref, rk_ref, lg_ref, lb_ref,
                 o_ref, h_sc, *, chunks_per_seq):
    C = CHUNK

    @pl.when(pl.program_id(0) % chunks_per_seq == 0)
    def _():
        h_sc[...] = jnp.zeros_like(h_sc)

    row = lax.broadcasted_iota(jnp.int32, (C, GROUP), 0)
    col = lax.broadcasted_iota(jnp.int32, (C, GROUP), 1) % HEAD
    low_strict = (row > col).astype(F32)
    low_incl = (row >= col).astype(F32)
    eye_sbs = (row == col).astype(F32)
    br = lax.broadcasted_iota(jnp.int32, (GROUP, GROUP), 0)
    bc = lax.broadcasted_iota(jnp.int32, (GROUP, GROUP), 1)
    bd_mask = (br // HEAD == bc // HEAD).astype(F32)
    bd_mask_b = bd_mask.astype(BF16)
    eye_bd = (br == bc).astype(F32)
    G = range(RW // GROUP)

    def bd(x):
        xb = x.astype(BF16)
        return jnp.concatenate([xb, xb, xb, xb], axis=0) * bd_mask_b

    def stack(a, b):
        return jnp.concatenate([a.astype(BF16), b.astype(BF16)], axis=0)

    tr = lax.broadcasted_iota(jnp.int32, (C, 3 * C), 0)
    tc = lax.broadcasted_iota(jnp.int32, (C, 3 * C), 1) % C
    tri3 = (tr >= tc).astype(BF16)
    lw_all = lw_ref[...]
    cum_all = jnp.dot(tri3, jnp.concatenate(_split3(lw_all), axis=0), preferred_element_type=F32)
    tot_all = cum_all[C - 1:C]
    e_neg = jnp.exp(-cum_all)
    e_end = jnp.exp(tot_all - cum_all)
    kk_all = kk_ref[...]
    b_all = kk_all * lr_ref[...]
    k_all = k_ref[...]
    v_all = v_ref[...]
    at_all = -kk_all * jnp.exp(cum_all - lw_all)
    rt_all = r_ref[...] * jnp.exp(cum_all)
    bt_all = b_all * e_neg
    kt_all = k_all * e_neg
    be_all = (b_all * e_end).astype(BF16)
    ke_all = (k_all * e_end).astype(BF16)
    gamma_all = jnp.exp(tot_all)

    sl = [slice(gi * GROUP, (gi + 1) * GROUP) for gi in G]
    ar = [stack(at_all[:, s], rt_all[:, s]) for s in sl]
    p_b = [_bdot(ar[g], bd(bt_all[:, sl[g]]), NT) for g in G]
    p_k = [_bdot(ar[g], bd(kt_all[:, sl[g]]), NT) for g in G]
    l_ab = [p_b[g][:C] * low_strict for g in G]
    m_rb = [(p_b[g][C:] * low_incl).astype(BF16) for g in G]
    l_ak = [p_k[g][:C] * low_strict for g in G]
    m_rk = [p_k[g][C:] * low_incl for g in G]

    t_m = [eye_sbs + l_ab[g] for g in G]
    p = [_bdot(l_ab[g], bd(l_ab[g])) for g in G]
    for _ in range(4):
        both = [_bdot(stack(p[g], t_m[g]), bd(p[g])) for g in G]
        p = [both[g][:C] for g in G]
        t_m = [t_m[g] + both[g][C:] for g in G]
    t_m = [(t_m[g] + _bdot(t_m[g], bd(p[g]))).astype(BF16) for g in G]

    bd_v = [bd(v_all[:, s]) for s in sl]
    lv = [_bdot(l_ak[g], bd_v[g]) for g in G]
    a_h = [_bdot(t_m[g], bd(at_all[:, sl[g]])).astype(BF16) for g in G]
    u0 = [_bdot(t_m[g], bd(lv[g])).astype(BF16) for g in G]
    r_h = [rt_all[:, sl[g]] + _bdot(m_rb[g], bd(a_h[g])) for g in G]
    y0 = [_bdot(m_rb[g], bd(u0[g])) + _bdot(m_rk[g], bd_v[g]) for g in G]
    phi = [(eye_bd * gamma_all[:, sl[g]] + _bdot(be_all[:, sl[g]], a_h[g], TN)) * bd_mask for g in G]
    psi = [(_bdot(be_all[:, sl[g]], u0[g], TN) + _bdot(ke_all[:, sl[g]], v_all[:, sl[g]], TN)) * bd_mask
           for g in G]
    h0 = [h_sc[g] for g in G]
    ys = [_bdot(r_h[g], h0[g]) + y0[g] for g in G]
    for g in G:
        h_sc[g] = _bdot(phi[g], h0[g]) + psi[g]

    y = jnp.concatenate(ys, axis=1)
    ones_pair = _ones_pair()
    m = _head_sum(y, ones_pair) * (1.0 / HEAD)
    d = y - m
    var = _head_sum(d * d, ones_pair) * (1.0 / HEAD)
    yn = d * lax.rsqrt(var + GN_EPS) * lg_ref[...] + lb_ref[...]
    bonus = _head_sum(r_ref[...] * k_ref[...] * rk_ref[...], ones_pair) * v_ref[...]
    o_ref[...] = ((yn + bonus) * g_ref[...]).astype(o_ref.dtype)


def _scan(r, lw, k, v, kkn, lr, g, r_k, lnx_g, lnx_b):
    M = r.shape[0]
    row = pl.BlockSpec((CHUNK, RW), lambda i: (i, 0))
    vec = pl.BlockSpec((1, RW), lambda i: (0, 0))
    kern = functools.partial(_scan_kernel, chunks_per_seq=SEQ // CHUNK)
    return pl.pallas_call(
        kern,
        grid=(M // CHUNK,),
        in_specs=[row] * 7 + [vec] * 3,
        out_specs=row,
        out_shape=jax.ShapeDtypeStruct((M, RW), BF16),
        scratch_shapes=[pltpu.VMEM((RW // GROUP, GROUP, GROUP), F32)],
        compiler_params=_cparams("arbitrary"),
        name="rwkv_scan",
    )(r, lw, k, v, kkn, lr, g, r_k, lnx_g, lnx_b)


def _fox_gate_kernel(fl_ref, bf_ref, c_ref, carry, *, blocks_per_seq):
    @pl.when(pl.program_id(0) % blocks_per_seq == 0)
    def _():
        carry[...] = jnp.zeros_like(carry)

    bm = fl_ref.shape[0]
    lf = jax.nn.log_sigmoid(fl_ref[...] + bf_ref[...])
    tr = lax.broadcasted_iota(jnp.int32, (bm, 3 * bm), 0)
    tc = lax.broadcasted_iota(jnp.int32, (bm, 3 * bm), 1) % bm
    tri3 = (tr >= tc).astype(BF16)
    c = jnp.dot(tri3, jnp.concatenate(_split3(lf), axis=0), preferred_element_type=F32) + carry[...]
    c_ref[...] = c
    carry[...] = c[bm - 1:bm]


def _fox_gate(fl, b_f, bm=512):
    M = fl.shape[0]
    kern = functools.partial(_fox_gate_kernel, blocks_per_seq=SEQ // bm)
    return pl.pallas_call(
        kern,
        grid=(M // bm,),
        in_specs=[pl.BlockSpec((bm, LANES), lambda m: (m, 0)),
                  pl.BlockSpec((1, LANES), lambda m: (0, 0))],
        out_specs=pl.BlockSpec((bm, LANES), lambda m: (m, 0)),
        out_shape=jax.ShapeDtypeStruct((M, LANES), F32),
        scratch_shapes=[pltpu.VMEM((1, LANES), F32)],
        compiler_params=_cparams("arbitrary"),
        name="fox_gate",
    )(fl, b_f)


def _flash_kernel(q_ref, k_ref, v_ref, c_ref, o_ref, *, blk, heads):
    qi = pl.program_id(2)
    log2e = 1.4426950408889634
    sc2 = FOX_HEAD ** -0.5 * log2e
    nt = blk // LANES

    def block(j, state, masked):
        new = []
        for h in range(heads):
            m_old, l_old, acc = state[h]
            hs = slice(h * FOX_HEAD, (h + 1) * FOX_HEAD)
            rows = pl.ds(pl.multiple_of(j * blk, blk), blk)
            q = q_ref[0, :, hs]
            k = k_ref[0, rows, hs]
            v = v_ref[0, rows, hs]
            ck2 = c_ref[0, h, pl.ds(j, 1), :] * log2e
            s2 = lax.dot_general(q, k, NT, preferred_element_type=F32) * sc2 - ck2
            if masked:
                row = lax.broadcasted_iota(jnp.int32, (blk, blk), 0)
                col = lax.broadcasted_iota(jnp.int32, (blk, blk), 1)
                s2 = jnp.where(row >= col, s2, -jnp.inf)
            m_new = jnp.maximum(m_old, jnp.max(s2, axis=-1, keepdims=True))
            alpha = jnp.exp2(m_old - m_new)
            p = jnp.exp2(s2 - jnp.concatenate([m_new] * nt, axis=1))
            l_new = alpha * l_old + jnp.sum(p, axis=-1, keepdims=True)
            acc = alpha * acc + jnp.dot(p.astype(BF16), v, preferred_element_type=F32)
            new.append((m_new, l_new, acc))
        return tuple(new)

    init = tuple((jnp.full((blk, LANES), -jnp.inf, F32), jnp.zeros((blk, LANES), F32),
                  jnp.zeros((blk, FOX_HEAD), F32)) for _ in range(heads))
    state = lax.fori_loop(0, qi, lambda j, st: block(j, st, False), init)
    state = block(qi, state, True)
    for h in range(heads):
        _, l_fin, acc = state[h]
        o_ref[0, :, h * FOX_HEAD:(h + 1) * FOX_HEAD] = (acc / l_fin).astype(o_ref.dtype)


def _flash(qkv, c_blk, blk=512, heads=2):
    B, T, _ = qkv.shape
    hg = FOX_HEADS // heads
    w = heads * FOX_HEAD
    kern = functools.partial(_flash_kernel, blk=blk, heads=heads)
    return pl.pallas_call(
        kern,
        grid=(B, hg, T // blk),
        in_specs=[pl.BlockSpec((1, blk, w), lambda b, g, qi: (b, qi, g)),
                  pl.BlockSpec((1, T, w), lambda b, g, qi: (b, 0, hg + g)),
                  pl.BlockSpec((1, T, w), lambda b, g, qi: (b, 0, 2 * hg + g)),
                  pl.BlockSpec((1, heads, T // blk, blk), lambda b, g, qi: (b, g, 0, 0))],
        out_specs=pl.BlockSpec((1, blk, w), lambda b, g, qi: (b, qi, g)),
        out_shape=jax.ShapeDtypeStruct((B, T, D_MODEL), BF16),
        compiler_params=_cparams("parallel", "parallel", "arbitrary"),
        name="fox_flash",
    )(qkv, qkv, qkv, c_blk)


def _pad_cols(w, width):
    return jnp.pad(w, ((0, 0), (0, width - w.shape[1])))


def _pad_rows(w, height):
    return jnp.pad(w, ((0, height - w.shape[0]), (0, 0)))


def _even_layer(xb, w_in, mu, w0, w2, a0, a2, g2, k_k, k_a, r_k, lnx_g, lnx_b,
                conv_w, w_out, v_first, vres):
    rw3 = 3 * RW
    o_dw, o_da, o_dg = rw3, rw3 + DECAY_LORA, rw3 + DECAY_LORA + AAA_LORA
    n_rwkv = o_dg + GATE_LORA

    def lay(a):
        return jnp.concatenate([a[..., :rw3],
                                _pad_cols(a[..., o_dw:o_da], LANES),
                                _pad_cols(a[..., o_da:o_dg], LANES),
                                _pad_cols(a[..., o_dg:n_rwkv], 2 * LANES)], axis=-1)

    w_r = lay(w_in[:, :n_rwkv]).astype(BF16)
    w_c = w_in[:, n_rwkv:].astype(BF16)
    pr = _mm(xb, w_r, F32, bm=512, bn=RWKV_PAD // 7)
    y_conv = _sconv(xb, w_c, conv_w)
    vec = lambda a: a.reshape(1, -1)
    if vres is not None:
        v0, v1, v2 = vres
        vres_args = (vec(v0), _pad_cols(v1, LANES).astype(BF16), _pad_rows(v2, LANES).astype(BF16), v_first)
    else:
        vres_args = None
    r, lw, k, v, kkn, lr, g = _rwkv_prep(
        pr, lay(vec(mu)), vec(w0), _pad_rows(w2, LANES).astype(BF16), vec(a0),
        _pad_rows(a2, LANES).astype(BF16), _pad_rows(g2, 2 * LANES).astype(BF16),
        vec(k_k), vec(k_a), vres_args)
    if vres is None:
        v_first = v
    y_rwkv = _scan(r, lw, k, v, kkn, lr, g, vec(r_k), vec(lnx_g), vec(lnx_b))
    cat = jnp.concatenate([y_rwkv, y_conv], axis=-1)
    return _mm(cat, w_out.astype(BF16), F32, bm=1024, bn=512), v_first


def _odd_layer(xb, w_in, b_f, w_out):
    M = xb.shape[0]
    B = M // SEQ
    d3 = 3 * D_MODEL
    qkv = _mm(xb, w_in[:, :d3].astype(BF16), BF16, bm=1024, bn=512)
    fl = _mm(xb, _pad_cols(w_in[:, d3:], LANES).astype(BF16), F32, bm=1024, bn=LANES)
    c = _fox_gate(fl, _pad_cols(b_f.reshape(1, -1), LANES))
    c = c.reshape(B, SEQ, LANES)[:, :, :FOX_HEADS]
    blk = 512
    c_t = jnp.swapaxes(c, 1, 2).reshape(B, FOX_HEADS, SEQ // blk, blk)
    o = _flash(qkv.reshape(B, SEQ, d3), c_t, blk=blk)
    return _mm(o.reshape(M, D_MODEL), w_out.astype(BF16), F32, bm=1024, bn=512)


def _conv_ffn(xb, w_up, conv_w, conv_b, w_down):
    h = _ffn_up(xb, w_up.astype(BF16), conv_w, conv_b.reshape(1, -1))
    return _mm(h, w_down.astype(BF16), F32, bm=512, bn=512)


def kernel(x, ln_g, ln_b, ev_w_in, ev_mu, ev_w0, ev_w2, ev_a0, ev_a2, ev_g2, ev_k_k, ev_k_a, ev_r_k, ev_lnx_g, ev_lnx_b, ev_v0, ev_v1, ev_v2, ev_conv_w, ev_w_out, od_w_in, od_b_f, od_w_out, ff_w_up, ff_conv_w, ff_conv_b, ff_w_down):
    B, T, D = x.shape
    xf = x.reshape(B * T, D)
    xb = xf.astype(BF16)
    v_first = None
    for i in range(DEPTH):
        if i % 2 == 0:
            e = i // 2
            vres = (ev_v0[e - 1], ev_v1[e - 1], ev_v2[e - 1]) if e > 0 else None
            mix, v_first = _even_layer(xb, ev_w_in[e], ev_mu[e], ev_w0[e], ev_w2[e], ev_a0[e],
                                       ev_a2[e], ev_g2[e], ev_k_k[e], ev_k_a[e], ev_r_k[e],
                                       ev_lnx_g[e], ev_lnx_b[e], ev_conv_w[e], ev_w_out[e],
                                       v_first, vres)
        else:
            o = i // 2
            mix = _odd_layer(xb, od_w_in[o], od_b_f[o], od_w_out[o])
        xf, xb = _ln_res(xf, mix, ln_g[i, 0], ln_b[i, 0])
        ffn = _conv_ffn(xb, ff_w_up[i], ff_conv_w[i], ff_conv_b[i], ff_w_down[i])
        xf, xb = _ln_res(xf, ffn, ln_g[i, 1], ln_b[i, 1])
    return xf.reshape(B, T, D)
```

```python
import functools

import jax
import jax.numpy as jnp
from jax import lax
from jax.experimental import pallas as pl
from jax.experimental.pallas import tpu as pltpu

F32 = jnp.float32
BF16 = jnp.bfloat16

D_MODEL = 2048
SEQ = 4096
DEPTH = 4
RW = 1024
HEAD = 64
DECAY_LORA = 64
AAA_LORA = 64
MV_LORA = 32
GATE_LORA = 160
CW = 1024
FOX_HEADS = 16
FOX_HEAD = 128
D_FF = 5632
GN_EPS = 64e-5
LN_EPS = 1e-5
DN_ALPHA = (2 * DEPTH) ** 0.25
LOG2E = 1.4426950408889634
FOX_SCALE_LOG2 = FOX_HEAD ** -0.5 * LOG2E

LANES = 128
SUBLANES = 8
VMEM_LIMIT = 56 * 1024 * 1024

LORA_W = 3 * RW
LORA_A = LORA_W + LANES
LORA_G = LORA_A + LANES
RWKV_PAD = LORA_G + 2 * LANES

CHUNK = 64
GROUP = 4 * HEAD

NN = (((1,), (0,)), ((), ()))
NT = (((1,), (1,)), ((), ()))
TN = (((0,), (0,)), ((), ()))


def _cparams(*sem):
    return pltpu.CompilerParams(dimension_semantics=sem, vmem_limit_bytes=VMEM_LIMIT)


def _bdot(a, b, dims=NN):
    return lax.dot_general(a.astype(BF16), b.astype(BF16), dims, preferred_element_type=F32)


def _split3(x):
    hi = x.astype(BF16)
    r1 = x - hi.astype(F32)
    mid = r1.astype(BF16)
    lo = (r1 - mid.astype(F32)).astype(BF16)
    return hi, mid, lo


def _shift_rows(u, prev8, s):
    ext = jnp.concatenate([prev8, u], axis=0)
    return pltpu.roll(ext, s, axis=0)[SUBLANES:]


def _head_sum(x, ones_pair):
    outs = []
    for t in range(x.shape[1] // LANES):
        xt = x[:, t * LANES:(t + 1) * LANES]
        hi = xt.astype(BF16)
        lo = (xt - hi.astype(F32)).astype(BF16)
        outs.append(jnp.dot(jnp.concatenate([hi, lo], axis=1), ones_pair,
                            preferred_element_type=F32))
    return jnp.concatenate(outs, axis=1)


def _ones_pair():
    r = lax.broadcasted_iota(jnp.int32, (2 * LANES, LANES), 0)
    c = lax.broadcasted_iota(jnp.int32, (2 * LANES, LANES), 1)
    return jnp.where((r % LANES) // HEAD == c // HEAD, 1.0, 0.0).astype(BF16)


def _mm_kernel(x_ref, w_ref, s_ref, o_ref, wb):
    @pl.when(pl.program_id(1) == 0)
    def _():
        wb[...] = w_ref[...].astype(BF16)

    acc = jnp.dot(x_ref[...], wb[...], preferred_element_type=F32)
    o_ref[...] = (acc * s_ref[...]).astype(o_ref.dtype)


def _mm(x, w, out_dtype, bm, bn, n_cols=None, col_scale=None):
    M, K = x.shape
    N = w.shape[1] if n_cols is None else n_cols
    if col_scale is None:
        col_scale = jnp.ones((1, N), F32)
    return pl.pallas_call(
        _mm_kernel,
        grid=(N // bn, M // bm),
        in_specs=[pl.BlockSpec((bm, K), lambda n, m: (m, 0)),
                  pl.BlockSpec((K, bn), lambda n, m: (0, n)),
                  pl.BlockSpec((1, bn), lambda n, m: (0, n))],
        out_specs=pl.BlockSpec((bm, bn), lambda n, m: (m, n)),
        out_shape=jax.ShapeDtypeStruct((M, N), out_dtype),
        scratch_shapes=[pltpu.VMEM((K, bn), BF16)],
        compiler_params=_cparams("parallel", "arbitrary"),
        name="mm",
    )(x, w, col_scale)


def _deepnorm(x, y, g, b):
    z = DN_ALPHA * x + y
    mu = jnp.mean(z, axis=-1, keepdims=True)
    d = z - mu
    var = jnp.mean(d * d, axis=-1, keepdims=True)
    return d * lax.rsqrt(var + LN_EPS) * g + b


def _mm_ln_kernel(*refs, n_x):
    x_refs = refs[:n_x]
    w_ref, r_ref, g_ref, b_ref, o_ref, ob_ref = refs[n_x:]
    x = x_refs[0][...] if n_x == 1 else jnp.concatenate([xr[...] for xr in x_refs], axis=1)
    y = jnp.dot(x, w_ref[...], preferred_element_type=F32)
    o = _deepnorm(r_ref[...], y, g_ref[...], b_ref[...])
    o_ref[...] = o
    ob_ref[...] = o.astype(BF16)


def _mm_ln(xs, w, resid, g, b, bm=512):
    M, D = resid.shape
    row = pl.BlockSpec((bm, D), lambda m: (m, 0))
    vec = pl.BlockSpec((1, D), lambda m: (0, 0))
    kern = functools.partial(_mm_ln_kernel, n_x=len(xs))
    return pl.pallas_call(
        kern,
        grid=(M // bm,),
        in_specs=[pl.BlockSpec((bm, x.shape[1]), lambda m: (m, 0)) for x in xs]
        + [pl.BlockSpec(w.shape, lambda m: (0, 0)), row, vec, vec],
        out_specs=[row, row],
        out_shape=[jax.ShapeDtypeStruct((M, D), F32), jax.ShapeDtypeStruct((M, D), BF16)],
        compiler_params=_cparams("parallel"),
        name="mm_ln",
    )(*xs, w, resid, g.reshape(1, D), b.reshape(1, D))


def _ln_res_kernel(x_ref, y_ref, g_ref, b_ref, o_ref, ob_ref):
    o = _deepnorm(x_ref[...], y_ref[...], g_ref[...], b_ref[...])
    o_ref[...] = o
    ob_ref[...] = o.astype(BF16)


def _ln_res(x, y, g, b, bm=512):
    M, D = x.shape
    row = pl.BlockSpec((bm, D), lambda m: (m, 0))
    vec = pl.BlockSpec((1, D), lambda m: (0, 0))
    return pl.pallas_call(
        _ln_res_kernel,
        grid=(M // bm,),
        in_specs=[row, row, vec, vec],
        out_specs=[row, row],
        out_shape=[jax.ShapeDtypeStruct((M, D), F32), jax.ShapeDtypeStruct((M, D), BF16)],
        compiler_params=_cparams("parallel"),
        name="ln_res",
    )(x, y, g.reshape(1, D), b.reshape(1, D))


def _ffn_up_kernel(x_ref, wg_ref, wv_ref, cwg_ref, cwv_ref, cbg_ref, cbv_ref, h_ref,
                   wgb, wvb, carry_g, carry_v, *, blocks_per_seq):
    m = pl.program_id(1)
    bm = x_ref.shape[0]
    sub = 2 * LANES

    @pl.when(m == 0)
    def _():
        wgb[...] = wg_ref[...].astype(BF16)
        wvb[...] = wv_ref[...].astype(BF16)

    first = m % blocks_per_seq == 0
    x = x_ref[...]

    def branch(w_sc, cw_ref, cb_ref, carry, cs):
        u = jnp.dot(x, w_sc[:, cs], preferred_element_type=F32)
        prev = jnp.where(first, 0.0, carry[:, cs])
        cw = cw_ref[:, cs]
        y = (_shift_rows(u, prev, 2) * cw[0:1] + _shift_rows(u, prev, 1) * cw[1:2]
             + u * cw[2:3] + cb_ref[:, cs])
        carry[:, cs] = u[bm - SUBLANES:]
        return y

    for c in range(h_ref.shape[1] // sub):
        cs = slice(c * sub, (c + 1) * sub)
        gate = branch(wgb, cwg_ref, cbg_ref, carry_g, cs)
        val = branch(wvb, cwv_ref, cbv_ref, carry_v, cs)
        h_ref[:, cs] = (gate * jax.nn.sigmoid(gate) * val).astype(h_ref.dtype)


def _ffn_up(x, w_up, conv_w, conv_b, bm=1024, bn=512):
    M, K = x.shape
    nb = D_FF // bn
    kern = functools.partial(_ffn_up_kernel, blocks_per_seq=SEQ // bm)
    return pl.pallas_call(
        kern,
        grid=(nb, M // bm),
        in_specs=[pl.BlockSpec((bm, K), lambda n, m: (m, 0)),
                  pl.BlockSpec((K, bn), lambda n, m: (0, n)),
                  pl.BlockSpec((K, bn), lambda n, m: (0, n + nb)),
                  pl.BlockSpec((3, bn), lambda n, m: (0, n)),
                  pl.BlockSpec((3, bn), lambda n, m: (0, n + nb)),
                  pl.BlockSpec((1, bn), lambda n, m: (0, n)),
                  pl.BlockSpec((1, bn), lambda n, m: (0, n + nb))],
        out_specs=pl.BlockSpec((bm, bn), lambda n, m: (m, n)),
        out_shape=jax.ShapeDtypeStruct((M, D_FF), BF16),
        scratch_shapes=[pltpu.VMEM((K, bn), BF16), pltpu.VMEM((K, bn), BF16),
                        pltpu.VMEM((SUBLANES, bn), F32), pltpu.VMEM((SUBLANES, bn), F32)],
        compiler_params=_cparams("parallel", "arbitrary"),
        name="ffn_up",
    )(x, w_up, w_up, conv_w, conv_w, conv_b, conv_b)


def _sconv_kernel(x_ref, wb_ref, wc_ref, wh_ref, cw_ref, o_ref, wbb, wcb, whb, carry,
                  *, blocks_per_seq):
    m = pl.program_id(1)

    @pl.when(m == 0)
    def _():
        wbb[...] = wb_ref[...].astype(BF16)
        wcb[...] = wc_ref[...].astype(BF16)
        whb[...] = wh_ref[...].astype(BF16)

    x = x_ref[...]
    gb = jnp.dot(x, wbb[...], preferred_element_type=F32)
    gc = jnp.dot(x, wcb[...], preferred_element_type=F32)
    h = jnp.dot(x, whb[...], preferred_element_type=F32)
    u = gc * h
    prev = jnp.where(m % blocks_per_seq == 0, 0.0, carry[...])
    cw = cw_ref[...]
    y = _shift_rows(u, prev, 2) * cw[0:1] + _shift_rows(u, prev, 1) * cw[1:2] + u * cw[2:3]
    carry[...] = u[u.shape[0] - SUBLANES:]
    o_ref[...] = (gb * y).astype(o_ref.dtype)


def _sconv(x, w_conv, conv_w, bm=1024, bn=256):
    M, K = x.shape
    nb = CW // bn
    kern = functools.partial(_sconv_kernel, blocks_per_seq=SEQ // bm)
    return pl.pallas_call(
        kern,
        grid=(nb, M // bm),
        in_specs=[pl.BlockSpec((bm, K), lambda n, m: (m, 0)),
                  pl.BlockSpec((K, bn), lambda n, m: (0, n)),
                  pl.BlockSpec((K, bn), lambda n, m: (0, n + nb)),
                  pl.BlockSpec((K, bn), lambda n, m: (0, n + 2 * nb)),
                  pl.BlockSpec((3, bn), lambda n, m: (0, n))],
        out_specs=pl.BlockSpec((bm, bn), lambda n, m: (m, n)),
        out_shape=jax.ShapeDtypeStruct((M, CW), BF16),
        scratch_shapes=[pltpu.VMEM((K, bn), BF16)] * 3 + [pltpu.VMEM((SUBLANES, bn), F32)],
        compiler_params=_cparams("parallel", "arbitrary"),
        name="sconv",
    )(x, w_conv, w_conv, w_conv, conv_w)


def _rwkv_prep_kernel(*refs, has_vres, blocks_per_seq):
    if has_vres:
        (pr_ref, prev_ref, mu_ref, w0_ref, w2_ref, a0_ref, a2_ref, g2_ref, kk_ref, ka_ref,
         v0_ref, v1_ref, v2_ref, vf_ref,
         r_o, lw_o, k_o, v_o, kkn_o, lr_o, g_o) = refs
    else:
        (pr_ref, prev_ref, mu_ref, w0_ref, w2_ref, a0_ref, a2_ref, g2_ref, kk_ref, ka_ref,
         r_o, lw_o, k_o, v_o, kkn_o, lr_o, g_o) = refs
    first = pl.program_id(0) % blocks_per_seq == 0
    pr = pr_ref[...]
    prev = jnp.where(first, 0.0, prev_ref[...])
    x = pr + (_shift_rows(pr, prev, 1) - pr) * mu_ref[...]
    r = x[:, 0:RW]
    k = x[:, RW:2 * RW]
    v = x[:, 2 * RW:3 * RW]
    dw = x[:, LORA_W:LORA_W + LANES]
    da = x[:, LORA_A:LORA_A + LANES]
    dg = x[:, LORA_G:LORA_G + 2 * LANES]
    z = w0_ref[...] + _bdot(jnp.tanh(dw), w2_ref[...])
    lw = -jnp.exp(F32(-0.5)) * jax.nn.sigmoid(z)
    lr = jax.nn.sigmoid(a0_ref[...] + _bdot(da, a2_ref[...]))
    g = _bdot(jax.nn.sigmoid(dg), g2_ref[...])
    if has_vres:
        mix = jax.nn.sigmoid(v0_ref[...] + _bdot(_bdot(v, v1_ref[...]), v2_ref[...]))
        v = v + (vf_ref[...] - v) * mix
    kkr = k * kk_ref[...]
    ss = _head_sum(kkr * kkr, _ones_pair())
    kkn = kkr / jnp.maximum(jnp.sqrt(ss), 1e-12)
    k2 = k * (1.0 + (lr - 1.0) * ka_ref[...])
    r_o[...] = r
    lw_o[...] = lw
    k_o[...] = k2
    v_o[...] = v
    kkn_o[...] = kkn
    lr_o[...] = lr
    g_o[...] = g


def _rwkv_prep(pr, mu, w0, w2, a0, a2, g2, k_k, k_a, vres, bm=256):
    M = pr.shape[0]
    has_vres = vres is not None
    row = lambda w: pl.BlockSpec((bm, w), lambda m: (m, 0))
    full = lambda a: pl.BlockSpec(a.shape, lambda m: (0, 0))
    per8 = bm // SUBLANES
    args = [pr, pr, mu, w0, w2, a0, a2, g2, k_k, k_a]
    in_specs = [row(RWKV_PAD),
                pl.BlockSpec((SUBLANES, RWKV_PAD), lambda m: (jnp.maximum(m * per8 - 1, 0), 0)),
                full(mu), full(w0), full(w2), full(a0), full(a2), full(g2), full(k_k), full(k_a)]
    if has_vres:
        v0, v1, v2, vf = vres
        args += [v0, v1, v2, vf]
        in_specs += [full(v0), full(v1), full(v2), row(RW)]
    kern = functools.partial(_rwkv_prep_kernel, has_vres=has_vres, blocks_per_seq=SEQ // bm)
    return pl.pallas_call(
        kern,
        grid=(M // bm,),
        in_specs=in_specs,
        out_specs=[row(RW)] * 7,
        out_shape=[jax.ShapeDtypeStruct((M, RW), F32)] * 7,
        compiler_params=_cparams("parallel"),
        name="rwkv_prep",
    )(*args)


def _scan_kernel(r_ref, lw_ref, k_ref, v_ref, kk_ref, lr_ref, g_ref, rk_ref, lg_ref, lb_ref,
                 o_ref, h_sc, *, chunks_per_seq):
    C = CHUNK

    @pl.when(pl.program_id(0) % chunks_per_seq == 0)
    def _():
        h_sc[...] = jnp.zeros_like(h_sc)

    row = lax.broadcasted_iota(jnp.int32, (C, GROUP), 0)
    col = lax.broadcasted_iota(jnp.int32, (C, GROUP), 1) % HEAD
    low_strict = (row > col).astype(F32)
    low_incl = (row >= col).astype(F32)
    eye_sbs = (row == col).astype(F32)
    br = lax.broadcasted_iota(jnp.int32, (GROUP, GROUP), 0)
    bc = lax.broadcasted_iota(jnp.int32, (GROUP, GROUP), 1)
    bd_mask = (br // HEAD == bc // HEAD).astype(F32)
    bd_mask_b = bd_mask.astype(BF16)
    eye_bd = (br == bc).astype(F32)
    G = range(RW // GROUP)

    def bd(x):
        xb = x.astype(BF16)
        return jnp.concatenate([xb, xb, xb, xb], axis=0) * bd_mask_b

    def stack(a, b):
        return jnp.concatenate([a.astype(BF16), b.astype(BF16)], axis=0)

    tr = lax.broadcasted_iota(jnp.int32, (C, 3 * C), 0)
    tc = lax.broadcasted_iota(jnp.int32, (C, 3 * C), 1) % C
    tri3 = (tr >= tc).astype(BF16)
    lw_all = lw_ref[...]
    cum_all = jnp.dot(tri3, jnp.concatenate(_split3(lw_all), axis=0), preferred_element_type=F32)
    tot_all = cum_all[C - 1:C]
    e_neg = jnp.exp(-cum_all)
    e_end = jnp.exp(tot_all - cum_all)
    kk_all = kk_ref[...]
    b_all = kk_all * lr_ref[...]
    k_all = k_ref[...]
    v_all = v_ref[...]
    at_all = -kk_all * jnp.exp(cum_all - lw_all)
    rt_all = r_ref[...] * jnp.exp(cum_all)
    bt_all = b_all * e_neg
    kt_all = k_all * e_neg
    be_all = (b_all * e_end).astype(BF16)
    ke_all = (k_all * e_end).astype(BF16)
    gamma_all = jnp.exp(tot_all)

    sl = [slice(gi * GROUP, (gi + 1) * GROUP) for gi in G]
    ar = [stack(at_all[:, s], rt_all[:, s]) for s in sl]
    p_b = [_bdot(ar[g], bd(bt_all[:, sl[g]]), NT) for g in G]
    p_k = [_bdot(ar[g], bd(kt_all[:, sl[g]]), NT) for g in G]
    l_ab = [p_b[g][:C] * low_strict for g in G]
    m_rb = [(p_b[g][C:] * low_incl).astype(BF16) for g in G]
    l_ak = [p_k[g][:C] * low_strict for g in G]
    m_rk = [p_k[g][C:] * low_incl for g in G]

    t_m = [eye_sbs + l_ab[g] for g in G]
    p = [_bdot(l_ab[g], bd(l_ab[g])) for g in G]
    for _ in range(4):
        both = [_bdot(stack(p[g], t_m[g]), bd(p[g])) for g in G]
        p = [both[g][:C] for g in G]
        t_m = [t_m[g] + both[g][C:] for g in G]
    t_m = [(t_m[g] + _bdot(t_m[g], bd(p[g]))).astype(BF16) for g in G]

    bd_v = [bd(v_all[:, s]) for s in sl]
    lv = [_bdot(l_ak[g], bd_v[g]) for g in G]
    a_h = [_bdot(t_m[g], bd(at_all[:, sl[g]])).astype(BF16) for g in G]
    u0 = [_bdot(t_m[g], bd(lv[g])).astype(BF16) for g in G]
    r_h = [rt_all[:, sl[g]] + _bdot(m_rb[g], bd(a_h[g])) for g in G]
    y0 = [_bdot(m_rb[g], bd(u0[g])) + _bdot(m_rk[g], bd_v[g]) for g in G]
    phi = [(eye_bd * gamma_all[:, sl[g]] + _bdot(be_all[:, sl[g]], a_h[g], TN)) * bd_mask for g in G]
    psi = [(_bdot(be_all[:, sl[g]], u0[g], TN) + _bdot(ke_all[:, sl[g]], v_all[:, sl[g]], TN)) * bd_mask
           for g in G]
    h0 = [h_sc[g] for g in G]
    ys = [_bdot(r_h[g], h0[g]) + y0[g] for g in G]
    for g in G:
        h_sc[g] = _bdot(phi[g], h0[g]) + psi[g]

    y = jnp.concatenate(ys, axis=1)
    ones_pair = _ones_pair()
    m = _head_sum(y, ones_pair) * (1.0 / HEAD)
    d = y - m
    var = _head_sum(d * d, ones_pair) * (1.0 / HEAD)
    yn = d * lax.rsqrt(var + GN_EPS) * lg_ref[...] + lb_ref[...]
    bonus = _head_sum(r_ref[...] * k_ref[...] * rk_ref[...], ones_pair) * v_ref[...]
    o_ref[...] = ((yn + bonus) * g_ref[...]).astype(o_ref.dtype)


def _scan(r, lw, k, v, kkn, lr, g, r_k, lnx_g, lnx_b):
    M = r.shape[0]
    row = pl.BlockSpec((CHUNK, RW), lambda i: (i, 0))
    vec = pl.BlockSpec((1, RW), lambda i: (0, 0))
    kern = functools.partial(_scan_kernel, chunks_per_seq=SEQ // CHUNK)
    return pl.pallas_call(
        kern,
        grid=(M // CHUNK,),
        in_specs=[row] * 7 + [vec] * 3,
        out_specs=row,
        out_shape=jax.ShapeDtypeStruct((M, RW), BF16),
        scratch_shapes=[pltpu.VMEM((RW // GROUP, GROUP, GROUP), F32)],
        compiler_params=_cparams("arbitrary"),
        name="rwkv_scan",
    )(r, lw, k, v, kkn, lr, g, r_k, lnx_g, lnx_b)


def _fox_gate_kernel(fl_ref, bf_ref, c_ref, carry, *, blocks_per_seq):
    @pl.when(pl.program_id(0) % blocks_per_seq == 0)
    def _():
        carry[...] = jnp.zeros_like(carry)

    bm = fl_ref.shape[0]
    lf = jax.nn.log_sigmoid(fl_ref[...] + bf_ref[...])
    tr = lax.broadcasted_iota(jnp.int32, (bm, 3 * bm), 0)
    tc = lax.broadcasted_iota(jnp.int32, (bm, 3 * bm), 1) % bm
    tri3 = (tr >= tc).astype(BF16)
    c = jnp.dot(tri3, jnp.concatenate(_split3(lf), axis=0), preferred_element_type=F32) + carry[...]
    c_ref[...] = c
    carry[...] = c[bm - 1:bm]


def _fox_gate(fl, b_f, bm=512):
    M = fl.shape[0]
    kern = functools.partial(_fox_gate_kernel, blocks_per_seq=SEQ // bm)
    return pl.pallas_call(
        kern,
        grid=(M // bm,),
        in_specs=[pl.BlockSpec((bm, LANES), lambda m: (m, 0)),
                  pl.BlockSpec((1, LANES), lambda m: (0, 0))],
        out_specs=pl.BlockSpec((bm, LANES), lambda m: (m, 0)),
        out_shape=jax.ShapeDtypeStruct((M, LANES), F32),
        scratch_shapes=[pltpu.VMEM((1, LANES), F32)],
        compiler_params=_cparams("arbitrary"),
        name="fox_gate",
    )(fl, b_f)


def _flash_kernel(q_ref, k_ref, v_ref, c_ref, o_ref, *, blk, heads):
    qi = pl.program_id(2)
    nt = blk // LANES

    def block(j, state, masked):
        new = []
        for h in range(heads):
            m_old, l_old, acc = state[h]
            hs = slice(h * FOX_HEAD, (h + 1) * FOX_HEAD)
            rows = pl.ds(pl.multiple_of(j * blk, blk), blk)
            q = q_ref[0, :, hs]
            k = k_ref[0, rows, hs]
            v = v_ref[0, rows, hs]
            ck2 = c_ref[0, h, pl.ds(j, 1), :] * LOG2E
            s2 = lax.dot_general(q, k, NT, preferred_element_type=F32) - ck2
            if masked:
                row = lax.broadcasted_iota(jnp.int32, (blk, blk), 0)
                col = lax.broadcasted_iota(jnp.int32, (blk, blk), 1)
                s2 = jnp.where(row >= col, s2, -jnp.inf)
            m_new = jnp.maximum(m_old, jnp.max(s2, axis=-1, keepdims=True))
            alpha = jnp.exp2(m_old - m_new)
            p = jnp.exp2(s2 - jnp.concatenate([m_new] * nt, axis=1))
            l_new = alpha * l_old + jnp.sum(p, axis=-1, keepdims=True)
            acc = alpha * acc + jnp.dot(p.astype(BF16), v, preferred_element_type=F32)
            new.append((m_new, l_new, acc))
        return tuple(new)

    init = tuple((jnp.full((blk, LANES), -jnp.inf, F32), jnp.zeros((blk, LANES), F32),
                  jnp.zeros((blk, FOX_HEAD), F32)) for _ in range(heads))
    state = lax.fori_loop(0, qi, lambda j, st: block(j, st, False), init)
    state = block(qi, state, True)
    for h in range(heads):
        _, l_fin, acc = state[h]
        o_ref[0, :, h * FOX_HEAD:(h + 1) * FOX_HEAD] = (acc / l_fin).astype(o_ref.dtype)


def _flash(qkv, c_blk, blk=512, heads=4):
    B, T, _ = qkv.shape
    hg = FOX_HEADS // heads
    w = heads * FOX_HEAD
    kern = functools.partial(_flash_kernel, blk=blk, heads=heads)
    return pl.pallas_call(
        kern,
        grid=(B, hg, T // blk),
        in_specs=[pl.BlockSpec((1, blk, w), lambda b, g, qi: (b, qi, g)),
                  pl.BlockSpec((1, T, w), lambda b, g, qi: (b, 0, hg + g)),
                  pl.BlockSpec((1, T, w), lambda b, g, qi: (b, 0, 2 * hg + g)),
                  pl.BlockSpec((1, heads, T // blk, blk), lambda b, g, qi: (b, g, 0, 0))],
        out_specs=pl.BlockSpec((1, blk, w), lambda b, g, qi: (b, qi, g)),
        out_shape=jax.ShapeDtypeStruct((B, T, D_MODEL), BF16),
        compiler_params=_cparams("parallel", "parallel", "arbitrary"),
        name="fox_flash",
    )(qkv, qkv, qkv, c_blk)


def _pad_cols(w, width):
    return jnp.pad(w, ((0, 0), (0, width - w.shape[1])))


def _pad_rows(w, height):
    return jnp.pad(w, ((0, height - w.shape[0]), (0, 0)))


def _even_layer(xf, xb, w_in, mu, w0, w2, a0, a2, g2, k_k, k_a, r_k, lnx_g, lnx_b,
                conv_w, w_out, v_first, vres, ln_g, ln_b):
    rw3 = 3 * RW
    o_dw, o_da, o_dg = rw3, rw3 + DECAY_LORA, rw3 + DECAY_LORA + AAA_LORA
    n_rwkv = o_dg + GATE_LORA

    def lay(a):
        return jnp.concatenate([a[..., :rw3],
                                _pad_cols(a[..., o_dw:o_da], LANES),
                                _pad_cols(a[..., o_da:o_dg], LANES),
                                _pad_cols(a[..., o_dg:n_rwkv], 2 * LANES)], axis=-1)

    pr = _mm(xb, lay(w_in[:, :n_rwkv]), F32, bm=1024, bn=RWKV_PAD // 7)
    y_conv = _sconv(xb, w_in[:, n_rwkv:], conv_w)
    vec = lambda a: a.reshape(1, -1)
    if vres is not None:
        v0, v1, v2 = vres
        vres_args = (vec(v0), _pad_cols(v1, LANES).astype(BF16), _pad_rows(v2, LANES).astype(BF16), v_first)
    else:
        vres_args = None
    r, lw, k, v, kkn, lr, g = _rwkv_prep(
        pr, lay(vec(mu)), vec(w0), _pad_rows(w2, LANES).astype(BF16), vec(a0),
        _pad_rows(a2, LANES).astype(BF16), _pad_rows(g2, 2 * LANES).astype(BF16),
        vec(k_k), vec(k_a), vres_args)
    if vres is None:
        v_first = v
    y_rwkv = _scan(r, lw, k, v, kkn, lr, g, vec(r_k), vec(lnx_g), vec(lnx_b))
    xf, xb = _mm_ln([y_rwkv, y_conv], w_out.astype(BF16), xf, ln_g, ln_b)
    return xf, xb, v_first


def _odd_layer(xf, xb, w_in, b_f, w_out, ln_g, ln_b):
    M = xb.shape[0]
    B = M // SEQ
    d3 = 3 * D_MODEL
    k_scale = jnp.concatenate([jnp.ones((1, D_MODEL), F32),
                               jnp.full((1, D_MODEL), FOX_SCALE_LOG2, F32),
                               jnp.ones((1, D_MODEL), F32)], axis=1)
    qkv = _mm(xb, w_in, BF16, bm=1024, bn=1024, n_cols=d3, col_scale=k_scale)
    fl = _mm(xb, _pad_cols(w_in[:, d3:], LANES), F32, bm=1024, bn=LANES)
    c = _fox_gate(fl, _pad_cols(b_f.reshape(1, -1), LANES))
    c = c.reshape(B, SEQ, LANES)[:, :, :FOX_HEADS]
    blk = 512
    c_t =jnp.swapaxes(c, 1, 2).reshape(B, FOX_HEADS, SEQ // blk, blk)
    o = _flash(qkv.reshape(B, SEQ, d3), c_t, blk=blk)
    return _mm_ln([o.reshape(M, D_MODEL)], w_out.astype(BF16), xf, ln_g, ln_b)


def _conv_ffn(xf, xb, w_up, conv_w, conv_b, w_down, ln_g, ln_b):
    h = _ffn_up(xb, w_up, conv_w, conv_b.reshape(1, -1))
    y = _mm(h, w_down, F32, bm=512, bn=512)
    return _ln_res(xf, y, ln_g, ln_b)


def kernel(x, ln_g, ln_b, ev_w_in, ev_mu, ev_w0, ev_w2, ev_a0, ev_a2, ev_g2, ev_k_k, ev_k_a, ev_r_k, ev_lnx_g, ev_lnx_b, ev_v0, ev_v1, ev_v2, ev_conv_w, ev_w_out, od_w_in, od_b_f, od_w_out, ff_w_up, ff_conv_w, ff_conv_b, ff_w_down):
    B, T, D = x.shape
    xf = x.reshape(B * T, D)
    xb = xf.astype(BF16)
    v_first = None
    for i in range(DEPTH):
        if i % 2 == 0:
            e = i // 2
            vres = (ev_v0[e - 1], ev_v1[e - 1], ev_v2[e - 1]) if e > 0 else None
            xf, xb, v_first = _even_layer(xf, xb, ev_w_in[e], ev_mu[e], ev_w0[e], ev_w2[e], ev_a0[e],
                                          ev_a2[e], ev_g2[e], ev_k_k[e], ev_k_a[e], ev_r_k[e],
                                          ev_lnx_g[e], ev_lnx_b[e], ev_conv_w[e], ev_w_out[e],
                                          v_first, vres, ln_g[i, 0], ln_b[i, 0])
        else:
            o = i // 2
            xf, xb = _odd_layer(xf, xb, od_w_in[o], od_b_f[o], od_w_out[o], ln_g[i, 0], ln_b[i, 0])
        xf, xb = _conv_ffn(xf, xb, ff_w_up[i], ff_conv_w[i], ff_conv_b[i], ff_w_down[i],
                           ln_g[i, 1], ln_b[i, 1])
    return xf.reshape(B, T, D)
```

```python
import functools

import jax
import jax.numpy as jnp
from jax import lax
from jax.experimental import pallas as pl
from jax.experimental.pallas import tpu as pltpu

F32 = jnp.float32
BF16 = jnp.bfloat16

D_MODEL = 2048
SEQ = 4096
DEPTH = 4
RW = 1024
HEAD = 64
DECAY_LORA = 64
AAA_LORA = 64
MV_LORA = 32
GATE_LORA = 160
CW = 1024
FOX_HEADS = 16
FOX_HEAD = 128
D_FF = 5632
GN_EPS = 64e-5
LN_EPS = 1e-5
DN_ALPHA = (2 * DEPTH) ** 0.25
LOG2E = 1.4426950408889634
FOX_SCALE_LOG2 = FOX_HEAD ** -0.5 * LOG2E

LANES = 128
SUBLANES = 8
VMEM_LIMIT = 56 * 1024 * 1024

LORA_W = 3 * RW
LORA_A = LORA_W + LANES
LORA_G = LORA_A + LANES
RWKV_PAD = LORA_G + 2 * LANES

CHUNK = 64
GROUP = 4 * HEAD

NN = (((1,), (0,)), ((), ()))
NT = (((1,), (1,)), ((), ()))
TN = (((0,), (0,)), ((), ()))


def _cparams(*sem):
    return pltpu.CompilerParams(dimension_semantics=sem, vmem_limit_bytes=VMEM_LIMIT)


def _bdot(a, b, dims=NN):
    return lax.dot_general(a.astype(BF16), b.astype(BF16), dims, preferred_element_type=F32)


def _split3(x):
    hi = x.astype(BF16)
    r1 = x - hi.astype(F32)
    mid = r1.astype(BF16)
    lo = (r1 - mid.astype(F32)).astype(BF16)
    return hi, mid, lo


def _shift_rows(u, prev8, s):
    ext = jnp.concatenate([prev8, u], axis=0)
    return pltpu.roll(ext, s, axis=0)[SUBLANES:]


def _head_sum(x, ones_pair):
    outs = []
    for t in range(x.shape[1] // LANES):
        xt = x[:, t * LANES:(t + 1) * LANES]
        hi = xt.astype(BF16)
        lo = (xt - hi.astype(F32)).astype(BF16)
        outs.append(jnp.dot(jnp.concatenate([hi, lo], axis=1), ones_pair,
                            preferred_element_type=F32))
    return jnp.concatenate(outs, axis=1)


def _ones_pair():
    r = lax.broadcasted_iota(jnp.int32, (2 * LANES, LANES), 0)
    c = lax.broadcasted_iota(jnp.int32, (2 * LANES, LANES), 1)
    return jnp.where((r % LANES) // HEAD == c // HEAD, 1.0, 0.0).astype(BF16)


def _mm_kernel(x_ref, w_ref, s_ref, o_ref, wb):
    @pl.when(pl.program_id(1) == 0)
    def _():
        wb[...] = w_ref[...].astype(BF16)

    acc = jnp.dot(x_ref[...], wb[...], preferred_element_type=F32)
    o_ref[...] = (acc * s_ref[...]).astype(o_ref.dtype)


def _layer_spec(w, layer, block, index):
    if w.ndim == 2:
        return pl.BlockSpec(block, index)
    return pl.BlockSpec((None,) + block, lambda n, m: (layer,) + index(n, m))


def _mm(x, w, out_dtype, bm, bn, n_cols=None, col_scale=None, layer=None):
    M, K = x.shape
    N = w.shape[-1] if n_cols is None else n_cols
    if col_scale is None:
        col_scale = jnp.ones((1, N), F32)
    return pl.pallas_call(
        _mm_kernel,
        grid=(N // bn, M // bm),
        in_specs=[pl.BlockSpec((bm, K), lambda n, m: (m, 0)),
                  _layer_spec(w, layer, (K, bn), lambda n, m: (0, n)),
                  pl.BlockSpec((1, bn), lambda n, m: (0, n))],
        out_specs=pl.BlockSpec((bm, bn), lambda n, m: (m, n)),
        out_shape=jax.ShapeDtypeStruct((M, N), out_dtype),
        scratch_shapes=[pltpu.VMEM((K, bn), BF16)],
        compiler_params=_cparams("parallel", "arbitrary"),
        name="mm",
    )(x, w, col_scale)


def _deepnorm(x, y, g, b):
    z = DN_ALPHA * x + y
    mu = jnp.mean(z, axis=-1, keepdims=True)
    d = z - mu
    var = jnp.mean(d * d, axis=-1, keepdims=True)
    return d * lax.rsqrt(var + LN_EPS) * g + b


def _mm_ln_kernel(*refs, n_x):
    x_refs = refs[:n_x]
    w_ref, r_ref, g_ref, b_ref, o_ref, ob_ref = refs[n_x:]
    x = x_refs[0][...] if n_x == 1 else jnp.concatenate([xr[...] for xr in x_refs], axis=1)
    y = jnp.dot(x, w_ref[...], preferred_element_type=F32)
    o = _deepnorm(r_ref[...], y, g_ref[...], b_ref[...])
    o_ref[...] = o
    ob_ref[...] = o.astype(BF16)


def _mm_ln(xs, w, resid, g, b, bm=512):
    M, D = resid.shape
    row = pl.BlockSpec((bm, D), lambda m: (m, 0))
    vec = pl.BlockSpec((1, D), lambda m: (0, 0))
    kern = functools.partial(_mm_ln_kernel, n_x=len(xs))
    return pl.pallas_call(
        kern,
        grid=(M // bm,),
        in_specs=[pl.BlockSpec((bm, x.shape[1]), lambda m: (m, 0)) for x in xs]
        + [pl.BlockSpec(w.shape, lambda m: (0, 0)), row, vec, vec],
        out_specs=[row, row],
        out_shape=[jax.ShapeDtypeStruct((M, D), F32), jax.ShapeDtypeStruct((M, D), BF16)],
        compiler_params=_cparams("parallel"),
        name="mm_ln",
    )(*xs, w, resid, g.reshape(1, D), b.reshape(1, D))


def _ln_res_kernel(x_ref, y_ref, g_ref, b_ref, o_ref, ob_ref):
    o = _deepnorm(x_ref[...], y_ref[...], g_ref[...], b_ref[...])
    o_ref[...] = o
    ob_ref[...] = o.astype(BF16)


def _ln_res(x, y, g, b, bm=512):
    M, D = x.shape
    row = pl.BlockSpec((bm, D), lambda m: (m, 0))
    vec = pl.BlockSpec((1, D), lambda m: (0, 0))
    return pl.pallas_call(
        _ln_res_kernel,
        grid=(M // bm,),
        in_specs=[row, row, vec, vec],
        out_specs=[row, row],
        out_shape=[jax.ShapeDtypeStruct((M, D), F32), jax.ShapeDtypeStruct((M, D), BF16)],
        compiler_params=_cparams("parallel"),
        name="ln_res",
    )(x, y, g.reshape(1, D), b.reshape(1, D))


def _ffn_up_kernel(x_ref, wg_ref, wv_ref, cwg_ref, cwv_ref, cbg_ref, cbv_ref, h_ref,
                   wgb, wvb, carry_g, carry_v, *, blocks_per_seq):
    m = pl.program_id(1)
    bm = x_ref.shape[0]
    sub = 2 * LANES

    @pl.when(m == 0)
    def _():
        wgb[...] = wg_ref[...].astype(BF16)
        wvb[...] = wv_ref[...].astype(BF16)

    first = m % blocks_per_seq == 0
    x = x_ref[...]

    def branch(w_sc, cw_ref, cb_ref, carry, cs):
        u = jnp.dot(x, w_sc[:, cs], preferred_element_type=F32)
        prev = jnp.where(first, 0.0, carry[:, cs])
        cw = cw_ref[:, cs]
        y = (_shift_rows(u, prev, 2) * cw[0:1] + _shift_rows(u, prev, 1) * cw[1:2]
             + u * cw[2:3] + cb_ref[:, cs])
        carry[:, cs] = u[bm - SUBLANES:]
        return y

    for c in range(h_ref.shape[1] // sub):
        cs = slice(c * sub, (c + 1) * sub)
        gate = branch(wgb, cwg_ref, cbg_ref, carry_g, cs)
        val = branch(wvb, cwv_ref, cbv_ref, carry_v, cs)
        h_ref[:, cs] = (gate * jax.nn.sigmoid(gate) * val).astype(h_ref.dtype)


def _ffn_up(x, w_up, conv_w, conv_b, layer, bm=1024, bn=512):
    M, K = x.shape
    nb = D_FF // bn
    kern = functools.partial(_ffn_up_kernel, blocks_per_seq=SEQ // bm)
    lo = lambda n, m: (0, n)
    hi = lambda n, m: (0, n + nb)
    return pl.pallas_call(
        kern,
        grid=(nb, M // bm),
        in_specs=[pl.BlockSpec((bm, K), lambda n, m: (m, 0)),
                  _layer_spec(w_up, layer, (K, bn), lo),
                  _layer_spec(w_up, layer, (K, bn), hi),
                  _layer_spec(conv_w, layer, (3, bn), lo),
                  _layer_spec(conv_w, layer, (3, bn), hi),
                  _layer_spec(conv_b, layer, (1, bn), lo),
                  _layer_spec(conv_b, layer, (1, bn), hi)],
        out_specs=pl.BlockSpec((bm, bn), lambda n, m: (m, n)),
        out_shape=jax.ShapeDtypeStruct((M, D_FF), BF16),
        scratch_shapes=[pltpu.VMEM((K, bn), BF16), pltpu.VMEM((K, bn), BF16),
                        pltpu.VMEM((SUBLANES, bn), F32), pltpu.VMEM((SUBLANES, bn), F32)],
        compiler_params=_cparams("parallel", "arbitrary"),
        name="ffn_up",
    )(x, w_up, w_up, conv_w, conv_w, conv_b, conv_b)


def _sconv_kernel(x_ref, wb_ref, wc_ref, wh_ref, cw_ref, o_ref, wbb, wcb, whb, carry,
                  *, blocks_per_seq):
    m = pl.program_id(1)

    @pl.when(m == 0)
    def _():
        wbb[...] = wb_ref[...].astype(BF16)
        wcb[...] = wc_ref[...].astype(BF16)
        whb[...] = wh_ref[...].astype(BF16)

    x = x_ref[...]
    gb = jnp.dot(x, wbb[...], preferred_element_type=F32)
    gc = jnp.dot(x, wcb[...], preferred_element_type=F32)
    h = jnp.dot(x, whb[...], preferred_element_type=F32)
    u = gc * h
    prev = jnp.where(m % blocks_per_seq == 0, 0.0, carry[...])
    cw = cw_ref[...]
    y = _shift_rows(u, prev, 2) * cw[0:1] + _shift_rows(u, prev, 1) * cw[1:2] + u * cw[2:3]
    carry[...] = u[u.shape[0] - SUBLANES:]
    o_ref[...] = (gb * y).astype(o_ref.dtype)


def _sconv(x, w_conv, conv_w, bm=1024, bn=256):
    M, K = x.shape
    nb = CW // bn
    kern = functools.partial(_sconv_kernel, blocks_per_seq=SEQ // bm)
    return pl.pallas_call(
        kern,
        grid=(nb, M // bm),
        in_specs=[pl.BlockSpec((bm, K), lambda n, m: (m, 0)),
                  pl.BlockSpec((K, bn), lambda n, m: (0, n)),
                  pl.BlockSpec((K, bn), lambda n, m: (0, n + nb)),
                  pl.BlockSpec((K, bn), lambda n, m: (0, n + 2 * nb)),
                  pl.BlockSpec((3, bn), lambda n, m: (0, n))],
        out_specs=pl.BlockSpec((bm, bn), lambda n, m: (m, n)),
        out_shape=jax.ShapeDtypeStruct((M, CW), BF16),
        scratch_shapes=[pltpu.VMEM((K, bn), BF16)] * 3 + [pltpu.VMEM((SUBLANES, bn), F32)],
        compiler_params=_cparams("parallel", "arbitrary"),
        name="sconv",
    )(x, w_conv, w_conv, w_conv, conv_w)


def _rwkv_prep_kernel(*refs, has_vres, blocks_per_seq):
    if has_vres:
        (pr_ref, prev_ref, mu_ref, w0_ref, w2_ref, a0_ref, a2_ref, g2_ref, kk_ref, ka_ref,
         v0_ref, v1_ref, v2_ref, vf_ref,
         r_o, lw_o, k_o, v_o, kkn_o, lr_o, g_o) = refs
    else:
        (pr_ref, prev_ref, mu_ref, w0_ref, w2_ref, a0_ref, a2_ref, g2_ref, kk_ref, ka_ref,
         r_o, lw_o, k_o, v_o, kkn_o, lr_o, g_o) = refs
    first = pl.program_id(0) % blocks_per_seq == 0
    pr = pr_ref[...]
    prev = jnp.where(first, 0.0, prev_ref[...])
    x = pr + (_shift_rows(pr, prev, 1) - pr) * mu_ref[...]
    r = x[:, 0:RW]
    k = x[:, RW:2 * RW]
    v = x[:, 2 * RW:3 * RW]
    dw = x[:, LORA_W:LORA_W + LANES]
    da = x[:, LORA_A:LORA_A + LANES]
    dg = x[:, LORA_G:LORA_G + 2 * LANES]
    z = w0_ref[...] + _bdot(jnp.tanh(dw), w2_ref[...])
    lw = -jnp.exp(F32(-0.5)) * jax.nn.sigmoid(z)
    lr = jax.nn.sigmoid(a0_ref[...] + _bdot(da, a2_ref[...]))
    g = _bdot(jax.nn.sigmoid(dg), g2_ref[...])
    if has_vres:
        mix = jax.nn.sigmoid(v0_ref[...] + _bdot(_bdot(v, v1_ref[...]), v2_ref[...]))
        v = v + (vf_ref[...] - v) * mix
    kkr = k * kk_ref[...]
    ss = _head_sum(kkr * kkr, _ones_pair())
    kkn = kkr / jnp.maximum(jnp.sqrt(ss), 1e-12)
    k2 = k * (1.0 + (lr - 1.0) * ka_ref[...])
    r_o[...] = r
    lw_o[...] = lw
    k_o[...] = k2
    v_o[...] = v
    kkn_o[...] = kkn
    lr_o[...] = lr
    g_o[...] = g


def _rwkv_prep(pr, mu, w0, w2, a0, a2, g2, k_k, k_a, vres, bm=256):
    M = pr.shape[0]
    has_vres = vres is not None
    row = lambda w: pl.BlockSpec((bm, w), lambda m: (m, 0))
    full = lambda a: pl.BlockSpec(a.shape, lambda m: (0, 0))
    per8 = bm // SUBLANES
    args = [pr, pr, mu, w0, w2, a0, a2, g2, k_k, k_a]
    in_specs = [row(RWKV_PAD),
                pl.BlockSpec((SUBLANES, RWKV_PAD), lambda m: (jnp.maximum(m * per8 - 1, 0), 0)),
                full(mu), full(w0), full(w2), full(a0), full(a2), full(g2), full(k_k), full(k_a)]
    if has_vres:
        v0, v1, v2, vf = vres
        args += [v0, v1, v2, vf]
        in_specs += [full(v0), full(v1), full(v2), row(RW)]
    kern = functools.partial(_rwkv_prep_kernel, has_vres=has_vres, blocks_per_seq=SEQ // bm)
    return pl.pallas_call(
        kern,
        grid=(M // bm,),
        in_specs=in_specs,
        out_specs=[row(RW)] * 7,
        out_shape=[jax.ShapeDtypeStruct((M, RW), F32)] * 7,
        compiler_params=_cparams("parallel"),
        name="rwkv_prep",
    )(*args)


def _scan_kernel(r_ref, lw_ref, k_ref, v_ref, kk_ref, lr_ref, g_ref, rk_ref, lg_ref, lb_ref,
                 o_ref, h_sc, *, chunks_per_seq):
    C = CHUNK

    @pl.when(pl.program_id(0) % chunks_per_seq == 0)
    def _():
        h_sc[...] = jnp.zeros_like(h_sc)

    row = lax.broadcasted_iota(jnp.int32, (C, GROUP), 0)
    col = lax.broadcasted_iota(jnp.int32, (C, GROUP), 1) % HEAD
    low_strict = (row > col).astype(F32)
    low_incl = (row >= col).astype(F32)
    eye_sbs = (row == col).astype(F32)
    br = lax.broadcasted_iota(jnp.int32, (GROUP, GROUP), 0)
    bc = lax.broadcasted_iota(jnp.int32, (GROUP, GROUP), 1)
    bd_mask = (br // HEAD == bc // HEAD).astype(F32)
    bd_mask_b = bd_mask.astype(BF16)
    eye_bd = (br == bc).astype(F32)
    G = range(RW // GROUP)

    def bd(x):
        xb = x.astype(BF16)
        return jnp.concatenate([xb, xb, xb, xb], axis=0) * bd_mask_b

    def stack(a, b):
        return jnp.concatenate([a.astype(BF16), b.astype(BF16)], axis=0)

    tr = lax.broadcasted_iota(jnp.int32, (C, 3 * C), 0)
    tc = lax.broadcasted_iota(jnp.int32, (C, 3 * C), 1) % C
    tri3 = (tr >= tc).astype(BF16)
    lw_all = lw_ref[...]
    cum_all = jnp.dot(tri3, jnp.concatenate(_split3(lw_all), axis=0), preferred_element_type=F32)
    tot_all = cum_all[C - 1:C]
    e_neg = jnp.exp(-cum_all)
    e_end = jnp.exp(tot_all - cum_all)
    kk_all = kk_ref[...]
    b_all = kk_all * lr_ref[...]
    k_all = k_ref[...]
    v_all = v_ref[...]
    at_all = -kk_all * jnp.exp(cum_all - lw_all)
    rt_all = r_ref[...] * jnp.exp(cum_all)
    bt_all = b_all * e_neg
    kt_all = k_all * e_neg
    be_all = (b_all * e_end).astype(BF16)
    ke_all = (k_all * e_end).astype(BF16)
    gamma_all = jnp.exp(tot_all)

    sl = [slice(gi * GROUP, (gi + 1) * GROUP) for gi in G]
    ar = [stack(at_all[:, s], rt_all[:, s]) for s in sl]
    p_b = [_bdot(ar[g], bd(bt_all[:, sl[g]]), NT) for g in G]
    p_k = [_bdot(ar[g], bd(kt_all[:, sl[g]]), NT) for g in G]
    l_ab = [p_b[g][:C] * low_strict for g in G]
    m_rb = [(p_b[g][C:] * low_incl).astype(BF16) for g in G]
    l_ak = [p_k[g][:C] * low_strict for g in G]
    m_rk = [p_k[g][C:] * low_incl for g in G]

    t_m = [eye_sbs + l_ab[g] for g in G]
    p = [_bdot(l_ab[g], bd(l_ab[g])) for g in G]
    for _ in range(4):
        both = [_bdot(stack(p[g], t_m[g]), bd(p[g])) for g in G]
        p = [both[g][:C] for g in G]
        t_m = [t_m[g] + both[g][C:] for g in G]
    t_m = [(t_m[g] + _bdot(t_m[g], bd(p[g]))).astype(BF16) for g in G]

    bd_v = [bd(v_all[:, s]) for s in sl]
    lv = [_bdot(l_ak[g], bd_v[g]) for g in G]
    a_h = [_bdot(t_m[g], bd(at_all[:, sl[g]])).astype(BF16) for g in G]
    u0 = [_bdot(t_m[g], bd(lv[g])).astype(BF16) for g in G]
    r_h = [rt_all[:, sl[g]] + _bdot(m_rb[g], bd(a_h[g])) for g in G]
    y0 = [_bdot(m_rb[g], bd(u0[g])) + _bdot(m_rk[g], bd_v[g]) for g in G]
    phi = [(eye_bd * gamma_all[:, sl[g]] + _bdot(be_all[:, sl[g]], a_h[g], TN)) * bd_mask for g in G]
    psi = [(_bdot(be_all[:, sl[g]], u0[g], TN) + _bdot(ke_all[:, sl[g]], v_all[:, sl[g]], TN)) * bd_mask
           for g in G]
    h0 = [h_sc[g] for g in G]
    ys = [_bdot(r_h[g], h0[g]) + y0[g] for g in G]
    for g in G:
        h_sc[g] = _bdot(phi[g], h0[g]) + psi[g]

    y = jnp.concatenate(ys, axis=1)
    ones_pair = _ones_pair()
    m = _head_sum(y, ones_pair) * (1.0 / HEAD)
    d = y - m
    var = _head_sum(d * d, ones_pair) * (1.0 / HEAD)
    yn = d * lax.rsqrt(var + GN_EPS) * lg_ref[...] + lb_ref[...]
    bonus = _head_sum(r_ref[...] * k_ref[...] * rk_ref[...], ones_pair) * v_ref[...]
    o_ref[...] = ((yn + bonus) * g_ref[...]).astype(o_ref.dtype)


def _scan(r, lw, k, v, kkn, lr, g, r_k, lnx_g, lnx_b):
    M = r.shape[0]
    row = pl.BlockSpec((CHUNK, RW), lambda i: (i, 0))
    vec = pl.BlockSpec((1, RW), lambda i: (0, 0))
    kern = functools.partial(_scan_kernel, chunks_per_seq=SEQ // CHUNK)
    return pl.pallas_call(
        kern,
        grid=(M // CHUNK,),
        in_specs=[row] * 7 + [vec] * 3,
        out_specs=row,
        out_shape=jax.ShapeDtypeStruct((M, RW), BF16),
        scratch_shapes=[pltpu.VMEM((RW // GROUP, GROUP, GROUP), F32)],
        compiler_params=_cparams("arbitrary"),
        name="rwkv_scan",
    )(r, lw, k, v, kkn, lr, g, r_k, lnx_g, lnx_b)


def _fox_gate_kernel(fl_ref, bf_ref, c_ref, carry, *, blocks_per_seq):
    @pl.when(pl.program_id(0) % blocks_per_seq == 0)
    def _():
        carry[...] = jnp.zeros_like(carry)

    bm = fl_ref.shape[0]
    lf = jax.nn.log_sigmoid(fl_ref[...] + bf_ref[...])
    tr = lax.broadcasted_iota(jnp.int32, (bm, 3 * bm), 0)
    tc = lax.broadcasted_iota(jnp.int32, (bm, 3 * bm), 1) % bm
    tri3 = (tr >= tc).astype(BF16)
    c = jnp.dot(tri3, jnp.concatenate(_split3(lf), axis=0), preferred_element_type=F32) + carry[...]
    c_ref[...] = c
    carry[...] = c[bm - 1:bm]


def _fox_gate(fl, b_f, bm=512):
    M = fl.shape[0]
    kern = functools.partial(_fox_gate_kernel, blocks_per_seq=SEQ // bm)
    return pl.pallas_call(
        kern,
        grid=(M // bm,),
        in_specs=[pl.BlockSpec((bm, LANES), lambda m: (m, 0)),
                  pl.BlockSpec((1, LANES), lambda m: (0, 0))],
        out_specs=pl.BlockSpec((bm, LANES), lambda m: (m, 0)),
        out_shape=jax.ShapeDtypeStruct((M, LANES), F32),
        scratch_shapes=[pltpu.VMEM((1, LANES), F32)],
        compiler_params=_cparams("arbitrary"),
        name="fox_gate",
    )(fl, b_f)


def _flash_kernel(q_ref, k_ref, v_ref, c_ref, o_ref, *, blk, heads):
    qi = pl.program_id(2)
    nt = blk // LANES

    def blocks(js, state):
        new = []
        for h in range(heads):
            m_old, l_old, acc = state[h]
            hs = slice(h * FOX_HEAD, (h + 1) * FOX_HEAD)
            q = q_ref[0, :, hs]
            s2s, vs = [], []
            for j, masked in js:
                rows = pl.ds(pl.multiple_of(j * blk, blk), blk)
                ck2 = c_ref[0, h, pl.ds(j, 1), :] * LOG2E
                s2 = lax.dot_general(q, k_ref[0, rows, hs], NT, preferred_element_type=F32) - ck2
                if masked:
                    row = lax.broadcasted_iota(jnp.int32, (blk, blk), 0)
                    col = lax.broadcasted_iota(jnp.int32, (blk, blk), 1)
                    s2 = jnp.where(row >= col, s2, -jnp.inf)
                s2s.append(s2)
                vs.append(v_ref[0, rows, hs])
            m_new = m_old
            for s2 in s2s:
                m_new = jnp.maximum(m_new, jnp.max(s2, axis=-1, keepdims=True))
            alpha = jnp.exp2(m_old - m_new)
            m_wide = jnp.concatenate([m_new] * nt, axis=1)
            l_new = alpha * l_old
            acc = alpha * acc
            for s2, v in zip(s2s, vs):
                p = jnp.exp2(s2 - m_wide)
                l_new = l_new + jnp.sum(p, axis=-1, keepdims=True)
                acc = acc + jnp.dot(p.astype(BF16), v, preferred_element_type=F32)
            new.append((m_new, l_new, acc))
        return tuple(new)

    init = tuple((jnp.full((blk, LANES), -jnp.inf, F32), jnp.zeros((blk, LANES), F32),
                  jnp.zeros((blk, FOX_HEAD), F32)) for _ in range(heads))
    state = lax.fori_loop(0, qi // 2,
                          lambda i, st: blocks([(2 * i, False), (2 * i + 1, False)], st), init)
    state = lax.cond(qi % 2 == 1,
                     lambda st: blocks([(qi - 1, False), (qi, True)], st),
                     lambda st: blocks([(qi, True)], st), state)
    for h in range(heads):
        _, l_fin, acc = state[h]
        o_ref[0, :, h * FOX_HEAD:(h + 1) * FOX_HEAD] = (acc / l_fin).astype(o_ref.dtype)


def _flash(qkv, c_blk, blk=512, heads=4):
    B, T, _ = qkv.shape
    hg = FOX_HEADS // heads
    w = heads * FOX_HEAD
    kern = functools.partial(_flash_kernel, blk=blk, heads=heads)
    return pl.pallas_call(
        kern,
        grid=(B, hg, T // blk),
        in_specs=[pl.BlockSpec((1, blk, w), lambda b, g, qi: (b, qi, g)),
                  pl.BlockSpec((1, T, w), lambda b, g, qi: (b, 0, hg + g)),
                  pl.BlockSpec((1, T, w), lambda b, g, qi: (b, 0, 2 * hg + g)),
                  pl.BlockSpec((1, heads, T // blk, blk), lambda b, g, qi: (b, g, 0, 0))],
        out_specs=pl.BlockSpec((1, blk, w), lambda b, g, qi: (b, qi, g)),
        out_shape=jax.ShapeDtypeStruct((B, T, D_MODEL), BF16),
        compiler_params=_cparams("parallel", "parallel", "arbitrary"),
        name="fox_flash",
    )(qkv, qkv, qkv, c_blk)


def _pad_cols(w, width):
    return jnp.pad(w, ((0, 0), (0, width - w.shape[1])))


def _pad_rows(w, height):
    return jnp.pad(w, ((0, height - w.shape[0]), (0, 0)))


def _even_layer(xf, xb, w_in, mu, w0, w2, a0, a2, g2, k_k, k_a, r_k, lnx_g, lnx_b,
                conv_w, w_out, v_first, vres, ln_g, ln_b):
    rw3 = 3 * RW
    o_dw, o_da, o_dg = rw3, rw3 + DECAY_LORA, rw3 + DECAY_LORA + AAA_LORA
    n_rwkv = o_dg + GATE_LORA

    def lay(a):
        return jnp.concatenate([a[..., :rw3],
                                _pad_cols(a[..., o_dw:o_da], LANES),
                                _pad_cols(a[..., o_da:o_dg], LANES),
                                _pad_cols(a[..., o_dg:n_rwkv], 2 * LANES)], axis=-1)

    pr = _mm(xb, lay(w_in[:, :n_rwkv]), F32, bm=1024, bn=RWKV_PAD // 7)
    y_conv = _sconv(xb, w_in[:, n_rwkv:], conv_w)
    vec = lambda a: a.reshape(1, -1)
    if vres is not None:
        v0, v1, v2 = vres
        vres_args = (vec(v0), _pad_cols(v1, LANES).astype(BF16), _pad_rows(v2, LANES).astype(BF16), v_first)
    else:
        vres_args = None
    r, lw, k, v, kkn, lr, g = _rwkv_prep(
        pr, lay(vec(mu)), vec(w0), _pad_rows(w2, LANES).astype(BF16), vec(a0),
        _pad_rows(a2, LANES).astype(BF16), _pad_rows(g2, 2 * LANES).astype(BF16),
        vec(k_k), vec(k_a), vres_args)
    if vres is None:
        v_first = v
    y_rwkv = _scan(r, lw, k, v, kkn, lr, g, vec(r_k), vec(lnx_g), vec(lnx_b))
    xf, xb = _mm_ln([y_rwkv, y_conv], w_out.astype(BF16), xf, ln_g, ln_b)
    return xf, xb, v_first


def _odd_layer(xf, xb, w_in, layer, b_f, w_out, ln_g, ln_b):
    M = xb.shape[0]
    B = M // SEQ
    d3 = 3 * D_MODEL
    k_scale = jnp.concatenate([jnp.ones((1, D_MODEL), F32),
                               jnp.full((1, D_MODEL), FOX_SCALE_LOG2, F32),
                               jnp.ones((1, D_MODEL), F32)], axis=1)
    qkv = _mm(xb, w_in, BF16, bm=1024, bn=1024, n_cols=d3, col_scale=k_scale, layer=layer)
    fl = _mm(xb, _pad_cols(w_in[layer, :, d3:], LANES), F32, bm=1024, bn=LANES)
    c = _fox_gate(fl, _pad_cols(b_f.reshape(1, -1), LANES))
    c = c.reshape(B, SEQ, LANES)[:, :, :FOX_HEADS]
    blk = 512
    c_t = jnp.swapaxes(c, 1, 2).reshape(B, FOX_HEADS, SEQ // blk, blk)
    o = _flash(qkv.reshape(B, SEQ, d3), c_t, blk=blk)
    return _mm_ln([o.reshape(M, D_MODEL)], w_out.astype(BF16), xf, ln_g, ln_b)


def _conv_ffn(xf, xb, w_up, conv_w, conv_b, w_down, layer, ln_g, ln_b):
    h = _ffn_up(xb, w_up, conv_w, conv_b.reshape(conv_b.shape[0], 1, -1), layer)
    y = _mm(h, w_down, F32, bm=512, bn=512, layer=layer)
    return _ln_res(xf, y, ln_g, ln_b)


def kernel(x, ln_g, ln_b, ev_w_in, ev_mu, ev_w0, ev_w2, ev_a0, ev_a2, ev_g2, ev_k_k, ev_k_a, ev_r_k, ev_lnx_g, ev_lnx_b, ev_v0, ev_v1, ev_v2, ev_conv_w, ev_w_out, od_w_in, od_b_f, od_w_out, ff_w_up, ff_conv_w, ff_conv_b, ff_w_down):
    B, T, D = x.shape
    xf = x.reshape(B * T, D)
    xb = xf.astype(BF16)
    v_first = None
    for i in range(DEPTH):
        if i % 2 == 0:
            e = i // 2
            vres = (ev_v0[e - 1], ev_v1[e - 1], ev_v2[e - 1]) if e > 0 else None
            xf, xb, v_first = _even_layer(xf, xb, ev_w_in[e], ev_mu[e], ev_w0[e], ev_w2[e], ev_a0[e],
                                          ev_a2[e], ev_g2[e], ev_k_k[e], ev_k_a[e], ev_r_k[e],
                                          ev_lnx_g[e], ev_lnx_b[e], ev_conv_w[e], ev_w_out[e],
                                          v_first, vres, ln_g[i, 0], ln_b[i, 0])
        else:
            o = i // 2
            xf, xb = _odd_layer(xf, xb, od_w_in, o, od_b_f[o], od_w_out[o], ln_g[i, 0], ln_b[i, 0])
        xf, xb = _conv_ffn(xf, xb, ff_w_up, ff_conv_w, ff_conv_b, ff_w_down, i,
                           ln_g[i, 1], ln_b[i, 1])
    return xf.reshape(B, T, D)
```

```python
import functools

import jax
import jax.numpy as jnp
from jax import lax
from jax.experimental import pallas as pl
from jax.experimental.pallas import tpu as pltpu

F32 = jnp.float32
BF16 = jnp.bfloat16

D_MODEL = 2048
SEQ = 4096
DEPTH = 4
RW = 1024
HEAD = 64
DECAY_LORA = 64
AAA_LORA = 64
MV_LORA = 32
GATE_LORA = 160
CW = 1024
FOX_HEADS = 16
FOX_HEAD = 128
D_FF = 5632
GN_EPS = 64e-5
LN_EPS = 1e-5
DN_ALPHA = (2 * DEPTH) ** 0.25
LOG2E = 1.4426950408889634
FOX_SCALE_LOG2 = FOX_HEAD ** -0.5 * LOG2E

LANES = 128
SUBLANES = 8
VMEM_LIMIT = 56 * 1024 * 1024

LORA_W = 3 * RW
LORA_A = LORA_W + LANES
LORA_G = LORA_A + LANES
RWKV_PAD = LORA_G + 2 * LANES

CHUNK = 64
GROUP = 4 * HEAD

NN = (((1,), (0,)), ((), ()))
NT = (((1,), (1,)), ((), ()))
TN = (((0,), (0,)), ((), ()))


def _cparams(*sem):
    return pltpu.CompilerParams(dimension_semantics=sem, vmem_limit_bytes=VMEM_LIMIT)


def _bdot(a, b, dims=NN):
    return lax.dot_general(a.astype(BF16), b.astype(BF16), dims, preferred_element_type=F32)


def _split3(x):
    hi = x.astype(BF16)
    r1 = x - hi.astype(F32)
    mid = r1.astype(BF16)
    lo = (r1 - mid.astype(F32)).astype(BF16)
    return hi, mid, lo


def _shift_rows(u, prev8, s):
    ext = jnp.concatenate([prev8, u], axis=0)
    return pltpu.roll(ext, s, axis=0)[SUBLANES:]


def _head_sum(x, ones_pair):
    outs = []
    for t in range(x.shape[1] // LANES):
        xt = x[:, t * LANES:(t + 1) * LANES]
        hi = xt.astype(BF16)
        lo = (xt - hi.astype(F32)).astype(BF16)
        outs.append(jnp.dot(jnp.concatenate([hi, lo], axis=1), ones_pair,
                            preferred_element_type=F32))
    return jnp.concatenate(outs, axis=1)


def _ones_pair():
    r = lax.broadcasted_iota(jnp.int32, (2 * LANES, LANES), 0)
    c = lax.broadcasted_iota(jnp.int32, (2 * LANES, LANES), 1)
    return jnp.where((r % LANES) // HEAD == c // HEAD, 1.0, 0.0).astype(BF16)


def _mm_kernel(x_ref, w_ref, s_ref, o_ref, wb):
    @pl.when(pl.program_id(1) == 0)
    def _():
        wb[...] = w_ref[...].astype(BF16)

    acc = jnp.dot(x_ref[...], wb[...], preferred_element_type=F32)
    o_ref[...] = (acc * s_ref[...]).astype(o_ref.dtype)


def _layer_spec(w, layer, block, index):
    if w.ndim == 2:
        return pl.BlockSpec(block, index)
    return pl.BlockSpec((None,) + block, lambda n, m: (layer,) + index(n, m))


def _mm(x, w, out_dtype, bm, bn, n_cols=None, col_scale=None, layer=None):
    M, K = x.shape
    N = w.shape[-1] if n_cols is None else n_cols
    if col_scale is None:
        col_scale = jnp.ones((1, N), F32)
    return pl.pallas_call(
        _mm_kernel,
        grid=(N // bn, M // bm),
        in_specs=[pl.BlockSpec((bm, K), lambda n, m: (m, 0)),
                  _layer_spec(w, layer, (K, bn), lambda n, m: (0, n)),
                  pl.BlockSpec((1, bn), lambda n, m: (0, n))],
        out_specs=pl.BlockSpec((bm, bn), lambda n, m: (m, n)),
        out_shape=jax.ShapeDtypeStruct((M, N), out_dtype),
        scratch_shapes=[pltpu.VMEM((K, bn), BF16)],
        compiler_params=_cparams("parallel", "arbitrary"),
        name="mm",
    )(x, w, col_scale)


def _deepnorm(x, y, g, b):
    z = DN_ALPHA * x + y
    mu = jnp.mean(z, axis=-1, keepdims=True)
    d = z - mu
    var = jnp.mean(d * d, axis=-1, keepdims=True)
    return d * lax.rsqrt(var + LN_EPS) * g + b


def _mm_ln_kernel(*refs, n_x):
    x_refs = refs[:n_x]
    w_ref, r_ref, g_ref, b_ref, o_ref, ob_ref = refs[n_x:]
    x = x_refs[0][...] if n_x == 1 else jnp.concatenate([xr[...] for xr in x_refs], axis=1)
    y = jnp.dot(x, w_ref[...], preferred_element_type=F32)
    o = _deepnorm(r_ref[...], y, g_ref[...], b_ref[...])
    o_ref[...] = o
    ob_ref[...] = o.astype(BF16)


def _mm_ln(xs, w, resid, g, b, bm=512):
    M, D = resid.shape
    row = pl.BlockSpec((bm, D), lambda m: (m, 0))
    vec = pl.BlockSpec((1, D), lambda m: (0, 0))
    kern = functools.partial(_mm_ln_kernel, n_x=len(xs))
    return pl.pallas_call(
        kern,
        grid=(M // bm,),
        in_specs=[pl.BlockSpec((bm, x.shape[1]), lambda m: (m, 0)) for x in xs]
        + [pl.BlockSpec(w.shape, lambda m: (0, 0), pipeline_mode=pl.Buffered(1)), row, vec, vec],
        out_specs=[row, row],
        out_shape=[jax.ShapeDtypeStruct((M, D), F32), jax.ShapeDtypeStruct((M, D), BF16)],
        compiler_params=_cparams("parallel"),
        name="mm_ln",
    )(*xs, w, resid, g.reshape(1, D), b.reshape(1, D))


def _ln_res_kernel(x_ref, y_ref, g_ref, b_ref, o_ref, ob_ref):
    o = _deepnorm(x_ref[...], y_ref[...], g_ref[...], b_ref[...])
    o_ref[...] = o
    ob_ref[...] = o.astype(BF16)


def _ln_res(x, y, g, b, bm=512):
    M, D = x.shape
    row = pl.BlockSpec((bm, D), lambda m: (m, 0))
    vec = pl.BlockSpec((1, D), lambda m: (0, 0))
    return pl.pallas_call(
        _ln_res_kernel,
        grid=(M // bm,),
        in_specs=[row, row, vec, vec],
        out_specs=[row, row],
        out_shape=[jax.ShapeDtypeStruct((M, D), F32), jax.ShapeDtypeStruct((M, D), BF16)],
        compiler_params=_cparams("parallel"),
        name="ln_res",
    )(x, y, g.reshape(1, D), b.reshape(1, D))


def _ffn_up_kernel(x_ref, wg_ref, wv_ref, cwg_ref, cwv_ref, cbg_ref, cbv_ref, h_ref,
                   wgb, wvb, carry_g, carry_v, *, blocks_per_seq):
    m = pl.program_id(1)
    bm = x_ref.shape[0]
    sub = 2 * LANES

    @pl.when(m == 0)
    def _():
        wgb[...] = wg_ref[...].astype(BF16)
        wvb[...] = wv_ref[...].astype(BF16)

    first = m % blocks_per_seq == 0
    x = x_ref[...]

    def branch(w_sc, cw_ref, cb_ref, carry, cs):
        u = jnp.dot(x, w_sc[:, cs], preferred_element_type=F32)
        prev = jnp.where(first, 0.0, carry[:, cs])
        cw = cw_ref[:, cs]
        y = (_shift_rows(u, prev, 2) * cw[0:1] + _shift_rows(u, prev, 1) * cw[1:2]
             + u * cw[2:3] + cb_ref[:, cs])
        carry[:, cs] = u[bm - SUBLANES:]
        return y

    for c in range(h_ref.shape[1] // sub):
        cs = slice(c * sub, (c + 1) * sub)
        gate = branch(wgb, cwg_ref, cbg_ref, carry_g, cs)
        val = branch(wvb, cwv_ref, cbv_ref, carry_v, cs)
        h_ref[:, cs] = (gate * jax.nn.sigmoid(gate) * val).astype(h_ref.dtype)


def _ffn_up(x, w_up, conv_w, conv_b, layer, bm=1024, bn=512):
    M, K = x.shape
    nb = D_FF // bn
    kern = functools.partial(_ffn_up_kernel, blocks_per_seq=SEQ // bm)
    lo = lambda n, m: (0, n)
    hi = lambda n, m: (0, n + nb)
    return pl.pallas_call(
        kern,
        grid=(nb, M // bm),
        in_specs=[pl.BlockSpec((bm, K), lambda n, m: (m, 0)),
                  _layer_spec(w_up, layer, (K, bn), lo),
                  _layer_spec(w_up, layer, (K, bn), hi),
                  _layer_spec(conv_w, layer, (3, bn), lo),
                  _layer_spec(conv_w, layer, (3, bn), hi),
                  _layer_spec(conv_b, layer, (1, bn), lo),
                  _layer_spec(conv_b, layer, (1, bn), hi)],
        out_specs=pl.BlockSpec((bm, bn), lambda n, m: (m, n)),
        out_shape=jax.ShapeDtypeStruct((M, D_FF), BF16),
        scratch_shapes=[pltpu.VMEM((K, bn), BF16), pltpu.VMEM((K, bn), BF16),
                        pltpu.VMEM((SUBLANES, bn), F32), pltpu.VMEM((SUBLANES, bn), F32)],
        compiler_params=_cparams("parallel", "arbitrary"),
        name="ffn_up",
    )(x, w_up, w_up, conv_w, conv_w, conv_b, conv_b)


def _sconv_kernel(x_ref, wb_ref, wc_ref, wh_ref, cw_ref, o_ref, wbb, wcb, whb, carry,
                  *, blocks_per_seq):
    m = pl.program_id(1)

    @pl.when(m == 0)
    def _():
        wbb[...] = wb_ref[...].astype(BF16)
        wcb[...] = wc_ref[...].astype(BF16)
        whb[...] = wh_ref[...].astype(BF16)

    x = x_ref[...]
    gb = jnp.dot(x, wbb[...], preferred_element_type=F32)
    gc = jnp.dot(x, wcb[...], preferred_element_type=F32)
    h = jnp.dot(x, whb[...], preferred_element_type=F32)
    u = gc * h
    prev = jnp.where(m % blocks_per_seq == 0, 0.0, carry[...])
    cw = cw_ref[...]
    y = _shift_rows(u, prev, 2) * cw[0:1] + _shift_rows(u, prev, 1) * cw[1:2] + u * cw[2:3]
    carry[...] = u[u.shape[0] - SUBLANES:]
    o_ref[...] = (gb * y).astype(o_ref.dtype)


def _sconv(x, w_conv, conv_w, bm=1024, bn=256):
    M, K = x.shape
    nb = CW // bn
    kern = functools.partial(_sconv_kernel, blocks_per_seq=SEQ // bm)
    return pl.pallas_call(
        kern,
        grid=(nb, M // bm),
        in_specs=[pl.BlockSpec((bm, K), lambda n, m: (m, 0)),
                  pl.BlockSpec((K, bn), lambda n, m: (0, n)),
                  pl.BlockSpec((K, bn), lambda n, m: (0, n + nb)),
                  pl.BlockSpec((K, bn), lambda n, m: (0, n + 2 * nb)),
                  pl.BlockSpec((3, bn), lambda n, m: (0, n))],
        out_specs=pl.BlockSpec((bm, bn), lambda n, m: (m, n)),
        out_shape=jax.ShapeDtypeStruct((M, CW), BF16),
        scratch_shapes=[pltpu.VMEM((K, bn), BF16)] * 3 + [pltpu.VMEM((SUBLANES, bn), F32)],
        compiler_params=_cparams("parallel", "arbitrary"),
        name="sconv",
    )(x, w_conv, w_conv, w_conv, conv_w)


def _rwkv_prep_kernel(*refs, has_vres, blocks_per_seq):
    if has_vres:
        (pr_ref, prev_ref, mu_ref, w0_ref, w2_ref, a0_ref, a2_ref, g2_ref, kk_ref, ka_ref,
         v0_ref, v1_ref, v2_ref, vf_ref,
         r_o, lw_o, k_o, v_o, kkn_o, lr_o, g_o) = refs
    else:
        (pr_ref, prev_ref, mu_ref, w0_ref, w2_ref, a0_ref, a2_ref, g2_ref, kk_ref, ka_ref,
         r_o, lw_o, k_o, v_o, kkn_o, lr_o, g_o) = refs
    first = pl.program_id(0) % blocks_per_seq == 0
    pr = pr_ref[...]
    prev = jnp.where(first, 0.0, prev_ref[...])
    x = pr + (_shift_rows(pr, prev, 1) - pr) * mu_ref[...]
    r = x[:, 0:RW]
    k = x[:, RW:2 * RW]
    v = x[:, 2 * RW:3 * RW]
    dw = x[:, LORA_W:LORA_W + LANES]
    da = x[:, LORA_A:LORA_A + LANES]
    dg = x[:, LORA_G:LORA_G + 2 * LANES]
    z = w0_ref[...] + _bdot(jnp.tanh(dw), w2_ref[...])
    lw = -jnp.exp(F32(-0.5)) * jax.nn.sigmoid(z)
    lr = jax.nn.sigmoid(a0_ref[...] + _bdot(da, a2_ref[...]))
    g = _bdot(jax.nn.sigmoid(dg), g2_ref[...])
    if has_vres:
        mix = jax.nn.sigmoid(v0_ref[...] + _bdot(_bdot(v, v1_ref[...]), v2_ref[...]))
        v = v + (vf_ref[...] - v) * mix
    kkr = k * kk_ref[...]
    ss = _head_sum(kkr * kkr, _ones_pair())
    kkn = kkr / jnp.maximum(jnp.sqrt(ss), 1e-12)
    k2 = k * (1.0 + (lr - 1.0) * ka_ref[...])
    r_o[...] = r
    lw_o[...] = lw
    k_o[...] = k2
    v_o[...] = v
    kkn_o[...] = kkn
    lr_o[...] = lr
    g_o[...] = g


def _rwkv_prep(pr, mu, w0, w2, a0, a2, g2, k_k, k_a, vres, bm=256):
    M = pr.shape[0]
    has_vres = vres is not None
    row = lambda w: pl.BlockSpec((bm, w), lambda m: (m, 0))
    full = lambda a: pl.BlockSpec(a.shape, lambda m: (0, 0))
    per8 = bm // SUBLANES
    args = [pr, pr, mu, w0, w2, a0, a2, g2, k_k, k_a]
    in_specs = [row(RWKV_PAD),
                pl.BlockSpec((SUBLANES, RWKV_PAD), lambda m: (jnp.maximum(m * per8 - 1, 0), 0)),
                full(mu), full(w0), full(w2), full(a0), full(a2), full(g2), full(k_k), full(k_a)]
    if has_vres:
        v0, v1, v2, vf = vres
        args += [v0, v1, v2, vf]
        in_specs += [full(v0), full(v1), full(v2), row(RW)]
    kern = functools.partial(_rwkv_prep_kernel, has_vres=has_vres, blocks_per_seq=SEQ // bm)
    return pl.pallas_call(
        kern,
        grid=(M // bm,),
        in_specs=in_specs,
        out_specs=[row(RW)] * 7,
        out_shape=[jax.ShapeDtypeStruct((M, RW), F32)] * 7,
        compiler_params=_cparams("parallel"),
        name="rwkv_prep",
    )(*args)


def _scan_kernel(r_ref, lw_ref, k_ref, v_ref, kk_ref, lr_ref, g_ref, rk_ref, lg_ref, lb_ref,
                 o_ref, h_sc, *, chunks_per_seq):
    C = CHUNK

    @pl.when(pl.program_id(0) % chunks_per_seq == 0)
    def _():
        h_sc[...] = jnp.zeros_like(h_sc)

    row = lax.broadcasted_iota(jnp.int32, (C, GROUP), 0)
    col = lax.broadcasted_iota(jnp.int32, (C, GROUP), 1) % HEAD
    low_strict = (row > col).astype(F32)
    low_incl = (row >= col).astype(F32)
    eye_sbs = (row == col).astype(F32)
    br = lax.broadcasted_iota(jnp.int32, (GROUP, GROUP), 0)
    bc = lax.broadcasted_iota(jnp.int32, (GROUP, GROUP), 1)
    bd_mask = (br // HEAD == bc // HEAD).astype(F32)
    bd_mask_b = bd_mask.astype(BF16)
    eye_bd = (br == bc).astype(F32)
    G = range(RW // GROUP)

    def bd(x):
        xb = x.astype(BF16)
        return jnp.concatenate([xb, xb, xb, xb], axis=0) * bd_mask_b

    def stack(a, b):
        return jnp.concatenate([a.astype(BF16), b.astype(BF16)], axis=0)

    tr = lax.broadcasted_iota(jnp.int32, (C, 3 * C), 0)
    tc = lax.broadcasted_iota(jnp.int32, (C, 3 * C), 1) % C
    tri3 = (tr >= tc).astype(BF16)
    lw_all = lw_ref[...]
    cum_all = jnp.dot(tri3, jnp.concatenate(_split3(lw_all), axis=0), preferred_element_type=F32)
    tot_all = cum_all[C - 1:C]
    e_neg = jnp.exp(-cum_all)
    e_end = jnp.exp(tot_all - cum_all)
    kk_all = kk_ref[...]
    b_all = kk_all * lr_ref[...]
    k_all = k_ref[...]
    v_all = v_ref[...]
    at_all = -kk_all * jnp.exp(cum_all - lw_all)
    rt_all = r_ref[...] * jnp.exp(cum_all)
    bt_all = b_all * e_neg
    kt_all = k_all * e_neg
    be_all = (b_all * e_end).astype(BF16)
    ke_all = (k_all * e_end).astype(BF16)
    gamma_all = jnp.exp(tot_all)

    sl = [slice(gi * GROUP, (gi + 1) * GROUP) for gi in G]
    ar = [stack(at_all[:, s], rt_all[:, s]) for s in sl]
    p_b = [_bdot(ar[g], bd(bt_all[:, sl[g]]), NT) for g in G]
    p_k = [_bdot(ar[g], bd(kt_all[:, sl[g]]), NT) for g in G]
    l_ab = [p_b[g][:C] * low_strict for g in G]
    m_rb = [(p_b[g][C:] * low_incl).astype(BF16) for g in G]
    l_ak = [p_k[g][:C] * low_strict for g in G]
    m_rk = [p_k[g][C:] * low_incl for g in G]

    t_m = [eye_sbs + l_ab[g] for g in G]
    p = [_bdot(l_ab[g], bd(l_ab[g])) for g in G]
    for _ in range(4):
        both = [_bdot(stack(p[g], t_m[g]), bd(p[g])) for g in G]
        p = [both[g][:C] for g in G]
        t_m = [t_m[g] + both[g][C:] for g in G]
    t_m = [(t_m[g] + _bdot(t_m[g], bd(p[g]))).astype(BF16) for g in G]

    bd_v = [bd(v_all[:, s]) for s in sl]
    lv = [_bdot(l_ak[g], bd_v[g]) for g in G]
    a_h = [_bdot(t_m[g], bd(at_all[:, sl[g]])).astype(BF16) for g in G]
    u0 = [_bdot(t_m[g], bd(lv[g])).astype(BF16) for g in G]
    r_h = [rt_all[:, sl[g]] + _bdot(m_rb[g], bd(a_h[g])) for g in G]
    y0 = [_bdot(m_rb[g], bd(u0[g])) + _bdot(m_rk[g], bd_v[g]) for g in G]
    phi = [(eye_bd * gamma_all[:, sl[g]] + _bdot(be_all[:, sl[g]], a_h[g], TN)) * bd_mask for g in G]
    psi = [(_bdot(be_all[:, sl[g]], u0[g], TN) + _bdot(ke_all[:, sl[g]], v_all[:, sl[g]], TN)) * bd_mask
           for g in G]
    h0 = [h_sc[g] for g in G]
    ys = [_bdot(r_h[g], h0[g]) + y0[g] for g in G]
    for g in G:
        h_sc[g] = _bdot(phi[g], h0[g]) + psi[g]

    y = jnp.concatenate(ys, axis=1)
    ones_pair = _ones_pair()
    m = _head_sum(y, ones_pair) * (1.0 / HEAD)
    d = y - m
    var = _head_sum(d * d, ones_pair) * (1.0 / HEAD)
    yn = d * lax.rsqrt(var + GN_EPS) * lg_ref[...] + lb_ref[...]
    bonus = _head_sum(r_ref[...] * k_ref[...] * rk_ref[...], ones_pair) * v_ref[...]
    o_ref[...] = ((yn + bonus) * g_ref[...]).astype(o_ref.dtype)


def _scan(r, lw, k, v, kkn, lr, g, r_k, lnx_g, lnx_b):
    M = r.shape[0]
    row = pl.BlockSpec((CHUNK, RW), lambda i: (i, 0))
    vec = pl.BlockSpec((1, RW), lambda i: (0, 0))
    kern = functools.partial(_scan_kernel, chunks_per_seq=SEQ // CHUNK)
    return pl.pallas_call(
        kern,
        grid=(M // CHUNK,),
        in_specs=[row] * 7 + [vec] * 3,
        out_specs=row,
        out_shape=jax.ShapeDtypeStruct((M, RW), BF16),
        scratch_shapes=[pltpu.VMEM((RW // GROUP, GROUP, GROUP), F32)],
        compiler_params=_cparams("arbitrary"),
        name="rwkv_scan",
    )(r, lw, k, v, kkn, lr, g, r_k, lnx_g, lnx_b)


def _fox_gate_kernel(fl_ref, bf_ref, c_ref, carry, *, blocks_per_seq):
    @pl.when(pl.program_id(0) % blocks_per_seq == 0)
    def _():
        carry[...] = jnp.zeros_like(carry)

    bm = fl_ref.shape[0]
    lf = jax.nn.log_sigmoid(fl_ref[...] + bf_ref[...])
    tr = lax.broadcasted_iota(jnp.int32, (bm, 3 * bm), 0)
    tc = lax.broadcasted_iota(jnp.int32, (bm, 3 * bm), 1) % bm
    tri3 = (tr >= tc).astype(BF16)
    c = jnp.dot(tri3, jnp.concatenate(_split3(lf), axis=0), preferred_element_type=F32) + carry[...]
    c_ref[...] = c
    carry[...] = c[bm - 1:bm]


def _fox_gate(fl, b_f, bm=512):
    M = fl.shape[0]
    kern = functools.partial(_fox_gate_kernel, blocks_per_seq=SEQ // bm)
    return pl.pallas_call(
        kern,
        grid=(M // bm,),
        in_specs=[pl.BlockSpec((bm, LANES), lambda m: (m, 0)),
                  pl.BlockSpec((1, LANES), lambda m: (0, 0))],
        out_specs=pl.BlockSpec((bm, LANES), lambda m: (m, 0)),
        out_shape=jax.ShapeDtypeStruct((M, LANES), F32),
        scratch_shapes=[pltpu.VMEM((1, LANES), F32)],
        compiler_params=_cparams("arbitrary"),
        name="fox_gate",
    )(fl, b_f)


def _flash_kernel(q_ref, k_ref, v_ref, c_ref, o_ref, *, blk, heads):
    qi = pl.program_id(2)
    nt = blk // LANES

    def blocks(js, state):
        new = []
        for h in range(heads):
            m_old, l_old, acc = state[h]
            hs = slice(h * FOX_HEAD, (h + 1) * FOX_HEAD)
            q = q_ref[0, :, hs]
            s2s, vs = [], []
            for j, masked in js:
                rows = pl.ds(pl.multiple_of(j * blk, blk), blk)
                ck2 = c_ref[0, h, pl.ds(j, 1), :] * LOG2E
                s2 = lax.dot_general(q, k_ref[0, rows, hs], NT, preferred_element_type=F32) - ck2
                if masked:
                    row = lax.broadcasted_iota(jnp.int32, (blk, blk), 0)
                    col = lax.broadcasted_iota(jnp.int32, (blk, blk), 1)
                    s2 = jnp.where(row >= col, s2, -jnp.inf)
                s2s.append(s2)
                vs.append(v_ref[0, rows, hs])
            m_new = m_old
            for s2 in s2s:
                m_new = jnp.maximum(m_new, jnp.max(s2, axis=-1, keepdims=True))
            alpha = jnp.exp2(m_old - m_new)
            m_wide = jnp.concatenate([m_new] * nt, axis=1)
            l_new = alpha * l_old
            acc = alpha * acc
            for s2, v in zip(s2s, vs):
                p = jnp.exp2(s2 - m_wide)
                l_new = l_new + jnp.sum(p, axis=-1, keepdims=True)
                acc = acc + jnp.dot(p.astype(BF16), v, preferred_element_type=F32)
            new.append((m_new, l_new, acc))
        return tuple(new)

    init = tuple((jnp.full((blk, LANES), -jnp.inf, F32), jnp.zeros((blk, LANES), F32),
                  jnp.zeros((blk, FOX_HEAD), F32)) for _ in range(heads))
    state = lax.fori_loop(0, qi // 2,
                          lambda i, st: blocks([(2 * i, False), (2 * i + 1, False)], st), init)
    state = lax.cond(qi % 2 == 1,
                     lambda st: blocks([(qi - 1, False), (qi, True)], st),
                     lambda st: blocks([(qi, True)], st), state)
    for h in range(heads):
        _, l_fin, acc = state[h]
        o_ref[0, :, h * FOX_HEAD:(h + 1) * FOX_HEAD] = (acc / l_fin).astype(o_ref.dtype)


def _flash(qkv, c_blk, blk=512, heads=4):
    B, T, _ = qkv.shape
    hg = FOX_HEADS // heads
    w = heads * FOX_HEAD
    kern = functools.partial(_flash_kernel, blk=blk, heads=heads)
    return pl.pallas_call(
        kern,
        grid=(B, hg, T // blk),
        in_specs=[pl.BlockSpec((1, blk, w), lambda b, g, qi: (b, qi, g)),
                  pl.BlockSpec((1, T, w), lambda b, g, qi: (b, 0, hg + g)),
                  pl.BlockSpec((1, T, w), lambda b, g, qi: (b, 0, 2 * hg + g)),
                  pl.BlockSpec((1, heads, T // blk, blk), lambda b, g, qi: (b, g, 0, 0))],
        out_specs=pl.BlockSpec((1, blk, w), lambda b, g, qi: (b, qi, g)),
        out_shape=jax.ShapeDtypeStruct((B, T, D_MODEL), BF16),
        compiler_params=_cparams("parallel", "parallel", "arbitrary"),
        name="fox_flash",
    )(qkv, qkv, qkv, c_blk)


def _pad_cols(w, width):
    return jnp.pad(w, ((0, 0), (0, width - w.shape[1])))


def _pad_rows(w, height):
    return jnp.pad(w, ((0, height - w.shape[0]), (0, 0)))


def _even_layer(xf, xb, w_in, mu, w0, w2, a0, a2, g2, k_k, k_a, r_k, lnx_g, lnx_b,
                conv_w, w_out, v_first, vres, ln_g, ln_b):
    rw3 = 3 * RW
    o_dw, o_da, o_dg = rw3, rw3 + DECAY_LORA, rw3 + DECAY_LORA + AAA_LORA
    n_rwkv = o_dg + GATE_LORA

    def lay(a):
        return jnp.concatenate([a[..., :rw3],
                                _pad_cols(a[..., o_dw:o_da], LANES),
                                _pad_cols(a[..., o_da:o_dg], LANES),
                                _pad_cols(a[..., o_dg:n_rwkv], 2 * LANES)], axis=-1)

    pr = _mm(xb, lay(w_in[:, :n_rwkv]), F32, bm=1024, bn=RWKV_PAD // 7)
    y_conv = _sconv(xb, w_in[:, n_rwkv:], conv_w)
    vec = lambda a: a.reshape(1, -1)
    if vres is not None:
        v0, v1, v2 = vres
        vres_args = (vec(v0), _pad_cols(v1, LANES).astype(BF16), _pad_rows(v2, LANES).astype(BF16), v_first)
    else:
        vres_args = None
    r, lw, k, v, kkn, lr, g = _rwkv_prep(
        pr, lay(vec(mu)), vec(w0), _pad_rows(w2, LANES).astype(BF16), vec(a0),
        _pad_rows(a2, LANES).astype(BF16), _pad_rows(g2, 2 * LANES).astype(BF16),
        vec(k_k), vec(k_a), vres_args)
    if vres is None:
        v_first = v
    y_rwkv = _scan(r, lw, k, v, kkn, lr, g, vec(r_k), vec(lnx_g), vec(lnx_b))
    xf, xb = _mm_ln([y_rwkv, y_conv], w_out.astype(BF16), xf, ln_g, ln_b)
    return xf, xb, v_first


def _odd_layer(xf, xb, w_in, layer, b_f, w_out, ln_g, ln_b):
    M = xb.shape[0]
    B = M // SEQ
    d3 = 3 * D_MODEL
    k_scale = jnp.concatenate([jnp.ones((1, D_MODEL), F32),
                               jnp.full((1, D_MODEL), FOX_SCALE_LOG2, F32),
                               jnp.ones((1, D_MODEL), F32)], axis=1)
    qkv = _mm(xb, w_in, BF16, bm=1024, bn=1024, n_cols=d3, col_scale=k_scale, layer=layer)
    fl = _mm(xb, _pad_cols(w_in[layer, :, d3:], LANES), F32, bm=1024, bn=LANES)
    c = _fox_gate(fl, _pad_cols(b_f.reshape(1, -1), LANES))
    c = c.reshape(B, SEQ, LANES)[:, :, :FOX_HEADS]
    blk = 512
    c_t = jnp.swapaxes(c, 1, 2).reshape(B, FOX_HEADS, SEQ // blk, blk)
    o = _flash(qkv.reshape(B, SEQ, d3), c_t, blk=blk)
    return _mm_ln([o.reshape(M, D_MODEL)], w_out.astype(BF16), xf, ln_g, ln_b)


def _conv_ffn(xf, xb, w_up, conv_w, conv_b, w_down, layer, ln_g, ln_b):
    h = _ffn_up(xb, w_up, conv_w, conv_b.reshape(conv_b.shape[0], 1, -1), layer)
    return _mm_ln([h], w_down[layer].astype(BF16), xf, ln_g, ln_b, bm=256)


def kernel(x, ln_g, ln_b, ev_w_in, ev_mu, ev_w0, ev_w2, ev_a0, ev_a2, ev_g2, ev_k_k, ev_k_a, ev_r_k, ev_lnx_g, ev_lnx_b, ev_v0, ev_v1, ev_v2, ev_conv_w, ev_w_out, od_w_in, od_b_f, od_w_out, ff_w_up, ff_conv_w, ff_conv_b, ff_w_down):
    B, T, D = x.shape
    xf = x.reshape(B * T, D)
    xb = xf.astype(BF16)
    v_first = None
    for i in range(DEPTH):
        if i % 2 == 0:
            e = i // 2
            vres = (ev_v0[e - 1], ev_v1[e - 1], ev_v2[e - 1]) if e > 0 else None
            xf, xb, v_first = _even_layer(xf, xb, ev_w_in[e], ev_mu[e], ev_w0[e], ev_w2[e], ev_a0[e],
                                          ev_a2[e], ev_g2[e], ev_k_k[e], ev_k_a[e], ev_r_k[e],
                                          ev_lnx_g[e], ev_lnx_b[e], ev_conv_w[e], ev_w_out[e],
                                          v_first, vres, ln_g[i, 0], ln_b[i, 0])
        else:
            o = i // 2
            xf, xb = _odd_layer(xf, xb, od_w_in, o, od_b_f[o], od_w_out[o], ln_g[i, 0], ln_b[i, 0])
        xf, xb = _conv_ffn(xf, xb, ff_w_up, ff_conv_w, ff_conv_b, ff_w_down, i,
                           ln_g[i, 1], ln_b[i, 1])
    return xf.reshape(B, T, D)
```

```python
import functools

import jax
import jax.numpy as jnp
from jax import lax
from jax.experimental import pallas as pl
from jax.experimental.pallas import tpu as pltpu

F32 = jnp.float32
BF16 = jnp.bfloat16

D_MODEL = 2048
SEQ = 4096
DEPTH = 4
RW = 1024
HEAD = 64
DECAY_LORA = 64
AAA_LORA = 64
MV_LORA = 32
GATE_LORA = 160
CW = 1024
FOX_HEADS = 16
FOX_HEAD = 128
D_FF = 5632
GN_EPS = 64e-5
LN_EPS = 1e-5
DN_ALPHA = (2 * DEPTH) ** 0.25
LOG2E = 1.4426950408889634
FOX_SCALE_LOG2 = FOX_HEAD ** -0.5 * LOG2E

LANES = 128
SUBLANES = 8
VMEM_LIMIT = 56 * 1024 * 1024

LORA_W = 3 * RW
LORA_A = LORA_W + LANES
LORA_G = LORA_A + LANES
RWKV_PAD = LORA_G + 2 * LANES

CHUNK = 64
GROUP = 4 * HEAD

NN = (((1,), (0,)), ((), ()))
NT = (((1,), (1,)), ((), ()))
TN = (((0,), (0,)), ((), ()))


def _cparams(*sem):
    return pltpu.CompilerParams(dimension_semantics=sem, vmem_limit_bytes=VMEM_LIMIT)


def _bdot(a, b, dims=NN):
    return lax.dot_general(a.astype(BF16), b.astype(BF16), dims, preferred_element_type=F32)


def _split3(x):
    hi = x.astype(BF16)
    r1 = x - hi.astype(F32)
    mid = r1.astype(BF16)
    lo = (r1 - mid.astype(F32)).astype(BF16)
    return hi, mid, lo


def _shift_rows(u, prev8, s):
    ext = jnp.concatenate([prev8, u], axis=0)
    return pltpu.roll(ext, s, axis=0)[SUBLANES:]


def _head_sum(x, ones_pair):
    outs = []
    for t in range(x.shape[1] // LANES):
        xt = x[:, t * LANES:(t + 1) * LANES]
        hi = xt.astype(BF16)
        lo = (xt - hi.astype(F32)).astype(BF16)
        outs.append(jnp.dot(jnp.concatenate([hi, lo], axis=1), ones_pair,
                            preferred_element_type=F32))
    return jnp.concatenate(outs, axis=1)


def _ones_pair():
    r = lax.broadcasted_iota(jnp.int32, (2 * LANES, LANES), 0)
    c = lax.broadcasted_iota(jnp.int32, (2 * LANES, LANES), 1)
    return jnp.where((r % LANES) // HEAD == c // HEAD, 1.0, 0.0).astype(BF16)


def _mm_kernel(x_ref, w_ref, s_ref, o_ref, wb):
    @pl.when(pl.program_id(1) == 0)
    def _():
        wb[...] = w_ref[...].astype(BF16)

    acc = jnp.dot(x_ref[...], wb[...], preferred_element_type=F32)
    o_ref[...] = (acc * s_ref[...]).astype(o_ref.dtype)


def _layer_spec(w, layer, block, index):
    if w.ndim == 2:
        return pl.BlockSpec(block, index)
    return pl.BlockSpec((None,) + block, lambda n, m: (layer,) + index(n, m))


def _mm(x, w, out_dtype, bm, bn, n_cols=None, col_scale=None, layer=None):
    M, K = x.shape
    N = w.shape[-1] if n_cols is None else n_cols
    if col_scale is None:
        col_scale = jnp.ones((1, N), F32)
    return pl.pallas_call(
        _mm_kernel,
        grid=(N // bn, M // bm),
        in_specs=[pl.BlockSpec((bm, K), lambda n, m: (m, 0)),
                  _layer_spec(w, layer, (K, bn), lambda n, m: (0, n)),
                  pl.BlockSpec((1, bn), lambda n, m: (0, n))],
        out_specs=pl.BlockSpec((bm, bn), lambda n, m: (m, n)),
        out_shape=jax.ShapeDtypeStruct((M, N), out_dtype),
        scratch_shapes=[pltpu.VMEM((K, bn), BF16)],
        compiler_params=_cparams("parallel", "arbitrary"),
        name="mm",
    )(x, w, col_scale)


def _deepnorm(x, y, g, b):
    z = DN_ALPHA * x + y
    mu = jnp.mean(z, axis=-1, keepdims=True)
    d = z - mu
    var = jnp.mean(d * d, axis=-1, keepdims=True)
    return d * lax.rsqrt(var + LN_EPS) * g + b


def _mm_ln_kernel(*refs, n_x):
    x_refs = refs[:n_x]
    w_ref, r_ref, g_ref, b_ref, o_ref, ob_ref = refs[n_x:]
    x = x_refs[0][...] if n_x == 1 else jnp.concatenate([xr[...] for xr in x_refs], axis=1)
    y = jnp.dot(x, w_ref[...], preferred_element_type=F32)
    o = _deepnorm(r_ref[...], y, g_ref[...], b_ref[...])
    o_ref[...] = o
    ob_ref[...] = o.astype(BF16)


def _mm_ln(xs, w, resid, g, b, bm=512, layer=None):
    M, D = resid.shape
    row = pl.BlockSpec((bm, D), lambda m: (m, 0))
    vec = pl.BlockSpec((1, D), lambda m: (0, 0))
    if w.ndim == 2:
        w_spec = pl.BlockSpec(w.shape, lambda m: (0, 0), pipeline_mode=pl.Buffered(1))
    else:
        w_spec = pl.BlockSpec((None,) + w.shape[1:], lambda m: (layer, 0, 0),
                              pipeline_mode=pl.Buffered(1))
    kern = functools.partial(_mm_ln_kernel, n_x=len(xs))
    return pl.pallas_call(
        kern,
        grid=(M // bm,),
        in_specs=[pl.BlockSpec((bm, x.shape[1]), lambda m: (m, 0)) for x in xs]
        + [w_spec, row, vec, vec],
        out_specs=[row, row],
        out_shape=[jax.ShapeDtypeStruct((M, D), F32), jax.ShapeDtypeStruct((M, D), BF16)],
        compiler_params=_cparams("parallel"),
        name="mm_ln",
    )(*xs, w, resid, g.reshape(1, D), b.reshape(1, D))


def _ln_res_kernel(x_ref, y_ref, g_ref, b_ref, o_ref, ob_ref):
    o = _deepnorm(x_ref[...], y_ref[...], g_ref[...], b_ref[...])
    o_ref[...] = o
    ob_ref[...] = o.astype(BF16)


def _ln_res(x, y, g, b, bm=512):
    M, D = x.shape
    row = pl.BlockSpec((bm, D), lambda m: (m, 0))
    vec = pl.BlockSpec((1, D), lambda m: (0, 0))
    return pl.pallas_call(
        _ln_res_kernel,
        grid=(M // bm,),
        in_specs=[row, row, vec, vec],
        out_specs=[row, row],
        out_shape=[jax.ShapeDtypeStruct((M, D), F32), jax.ShapeDtypeStruct((M, D), BF16)],
        compiler_params=_cparams("parallel"),
        name="ln_res",
    )(x, y, g.reshape(1, D), b.reshape(1, D))


def _ffn_up_kernel(x_ref, wg_ref, wv_ref, cwg_ref, cwv_ref, cbg_ref, cbv_ref, h_ref,
                   wgb, wvb, carry_g, carry_v, *, blocks_per_seq):
    m = pl.program_id(1)
    bm = x_ref.shape[0]
    sub = 2 * LANES

    @pl.when(m == 0)
    def _():
        wgb[...] = wg_ref[...].astype(BF16)
        wvb[...] = wv_ref[...].astype(BF16)

    first = m % blocks_per_seq == 0
    x = x_ref[...]

    def branch(w_sc, cw_ref, cb_ref, carry, cs):
        u = jnp.dot(x, w_sc[:, cs], preferred_element_type=F32)
        prev = jnp.where(first, 0.0, carry[:, cs])
        cw = cw_ref[:, cs]
        y = (_shift_rows(u, prev, 2) * cw[0:1] + _shift_rows(u, prev, 1) * cw[1:2]
             + u * cw[2:3] + cb_ref[:, cs])
        carry[:, cs] = u[bm - SUBLANES:]
        return y

    for c in range(h_ref.shape[1] // sub):
        cs = slice(c * sub, (c + 1) * sub)
        gate = branch(wgb, cwg_ref, cbg_ref, carry_g, cs)
        val = branch(wvb, cwv_ref, cbv_ref, carry_v, cs)
        h_ref[:, cs] = (gate * jax.nn.sigmoid(gate) * val).astype(h_ref.dtype)


def _ffn_up(x, w_up, conv_w, conv_b, layer, bm=1024, bn=512):
    M, K = x.shape
    nb = D_FF // bn
    kern = functools.partial(_ffn_up_kernel, blocks_per_seq=SEQ // bm)
    lo = lambda n, m: (0, n)
    hi = lambda n, m: (0, n + nb)
    return pl.pallas_call(
        kern,
        grid=(nb, M // bm),
        in_specs=[pl.BlockSpec((bm, K), lambda n, m: (m, 0)),
                  _layer_spec(w_up, layer, (K, bn), lo),
                  _layer_spec(w_up, layer, (K, bn), hi),
                  _layer_spec(conv_w, layer, (3, bn), lo),
                  _layer_spec(conv_w, layer, (3, bn), hi),
                  _layer_spec(conv_b, layer, (1, bn), lo),
                  _layer_spec(conv_b, layer, (1, bn), hi)],
        out_specs=pl.BlockSpec((bm, bn), lambda n, m: (m, n)),
        out_shape=jax.ShapeDtypeStruct((M, D_FF), BF16),
        scratch_shapes=[pltpu.VMEM((K, bn), BF16), pltpu.VMEM((K, bn), BF16),
                        pltpu.VMEM((SUBLANES, bn), F32), pltpu.VMEM((SUBLANES, bn), F32)],
        compiler_params=_cparams("parallel", "arbitrary"),
        name="ffn_up",
    )(x, w_up, w_up, conv_w, conv_w, conv_b, conv_b)


def _sconv_kernel(x_ref, wb_ref, wc_ref, wh_ref, cw_ref, o_ref, wbb, wcb, whb, carry,
                  *, blocks_per_seq):
    m = pl.program_id(1)

    @pl.when(m == 0)
    def _():
        wbb[...] = wb_ref[...].astype(BF16)
        wcb[...] = wc_ref[...].astype(BF16)
        whb[...] = wh_ref[...].astype(BF16)

    x = x_ref[...]
    gb = jnp.dot(x, wbb[...], preferred_element_type=F32)
    gc = jnp.dot(x, wcb[...], preferred_element_type=F32)
    h = jnp.dot(x, whb[...], preferred_element_type=F32)
    u = gc * h
    prev = jnp.where(m % blocks_per_seq == 0, 0.0, carry[...])
    cw = cw_ref[...]
    y = _shift_rows(u, prev, 2) * cw[0:1] + _shift_rows(u, prev, 1) * cw[1:2] + u * cw[2:3]
    carry[...] = u[u.shape[0] - SUBLANES:]
    o_ref[...] = (gb * y).astype(o_ref.dtype)


def _sconv(x, w_conv, conv_w, bm=1024, bn=256):
    M, K = x.shape
    nb = CW // bn
    kern = functools.partial(_sconv_kernel, blocks_per_seq=SEQ // bm)
    return pl.pallas_call(
        kern,
        grid=(nb, M // bm),
        in_specs=[pl.BlockSpec((bm, K), lambda n, m: (m, 0)),
                  pl.BlockSpec((K, bn), lambda n, m: (0, n)),
                  pl.BlockSpec((K, bn), lambda n, m: (0, n + nb)),
                  pl.BlockSpec((K, bn), lambda n, m: (0, n + 2 * nb)),
                  pl.BlockSpec((3, bn), lambda n, m: (0, n))],
        out_specs=pl.BlockSpec((bm, bn), lambda n, m: (m, n)),
        out_shape=jax.ShapeDtypeStruct((M, CW), BF16),
        scratch_shapes=[pltpu.VMEM((K, bn), BF16)] * 3 + [pltpu.VMEM((SUBLANES, bn), F32)],
        compiler_params=_cparams("parallel", "arbitrary"),
        name="sconv",
    )(x, w_conv, w_conv, w_conv, conv_w)


def _rwkv_prep_kernel(*refs, has_vres, blocks_per_seq):
    if has_vres:
        (pr_ref, prev_ref, mu_ref, w0_ref, w2_ref, a0_ref, a2_ref, g2_ref, kk_ref, ka_ref,
         v0_ref, v1_ref, v2_ref, vf_ref,
         r_o, lw_o, k_o, v_o, kkn_o, lr_o, g_o) = refs
    else:
        (pr_ref, prev_ref, mu_ref, w0_ref, w2_ref, a0_ref, a2_ref, g2_ref, kk_ref, ka_ref,
         r_o, lw_o, k_o, v_o, kkn_o, lr_o, g_o) = refs
    first = pl.program_id(0) % blocks_per_seq == 0
    pr = pr_ref[...]
    prev = jnp.where(first, 0.0, prev_ref[...])
    x = pr + (_shift_rows(pr, prev, 1) - pr) * mu_ref[...]
    r = x[:, 0:RW]
    k = x[:, RW:2 * RW]
    v = x[:, 2 * RW:3 * RW]
    dw = x[:, LORA_W:LORA_W + LANES]
    da = x[:, LORA_A:LORA_A + LANES]
    dg = x[:, LORA_G:LORA_G + 2 * LANES]
    z = w0_ref[...] + _bdot(jnp.tanh(dw), w2_ref[...])
    lw = -jnp.exp(F32(-0.5)) * jax.nn.sigmoid(z)
    lr = jax.nn.sigmoid(a0_ref[...] + _bdot(da, a2_ref[...]))
    g = _bdot(jax.nn.sigmoid(dg), g2_ref[...])
    if has_vres:
        mix = jax.nn.sigmoid(v0_ref[...] + _bdot(_bdot(v, v1_ref[...]), v2_ref[...]))
        v = v + (vf_ref[...] - v) * mix
    kkr = k * kk_ref[...]
    ss = _head_sum(kkr * kkr, _ones_pair())
    kkn = kkr / jnp.maximum(jnp.sqrt(ss), 1e-12)
    k2 = k * (1.0 + (lr - 1.0) * ka_ref[...])
    r_o[...] = r
    lw_o[...] = lw
    k_o[...] = k2
    v_o[...] = v
    kkn_o[...] = kkn
    lr_o[...] = lr
    g_o[...] = g


def _rwkv_prep(pr, mu, w0, w2, a0, a2, g2, k_k, k_a, vres, bm=256):
    M = pr.shape[0]
    has_vres = vres is not None
    row = lambda w: pl.BlockSpec((bm, w), lambda m: (m, 0))
    full = lambda a: pl.BlockSpec(a.shape, lambda m: (0, 0))
    per8 = bm // SUBLANES
    args = [pr, pr, mu, w0, w2, a0, a2, g2, k_k, k_a]
    in_specs = [row(RWKV_PAD),
                pl.BlockSpec((SUBLANES, RWKV_PAD), lambda m: (jnp.maximum(m * per8 - 1, 0), 0)),
                full(mu), full(w0), full(w2), full(a0), full(a2), full(g2), full(k_k), full(k_a)]
    if has_vres:
        v0, v1, v2, vf = vres
        args += [v0, v1, v2, vf]
        in_specs += [full(v0), full(v1), full(v2), row(RW)]
    kern = functools.partial(_rwkv_prep_kernel, has_vres=has_vres, blocks_per_seq=SEQ // bm)
    return pl.pallas_call(
        kern,
        grid=(M // bm,),
        in_specs=in_specs,
        out_specs=[row(RW)] * 7,
        out_shape=[jax.ShapeDtypeStruct((M, RW), F32)] * 7,
        compiler_params=_cparams("parallel"),
        name="rwkv_prep",
    )(*args)


def _scan_kernel(r_ref, lw_ref, k_ref, v_ref, kk_ref, lr_ref, g_ref, rk_ref, lg_ref, lb_ref,
                 o_ref, h_sc, *, chunks_per_seq):
    C = CHUNK

    @pl.when(pl.program_id(0) % chunks_per_seq == 0)
    def _():
        h_sc[...] = jnp.zeros_like(h_sc)

    row = lax.broadcasted_iota(jnp.int32, (C, GROUP), 0)
    col = lax.broadcasted_iota(jnp.int32, (C, GROUP), 1) % HEAD
    low_strict = (row > col).astype(F32)
    low_incl = (row >= col).astype(F32)
    eye_sbs = (row == col).astype(F32)
    br = lax.broadcasted_iota(jnp.int32, (GROUP, GROUP), 0)
    bc = lax.broadcasted_iota(jnp.int32, (GROUP, GROUP), 1)
    bd_mask = (br // HEAD == bc // HEAD).astype(F32)
    bd_mask_b = bd_mask.astype(BF16)
    eye_bd = (br == bc).astype(F32)
    G = range(RW // GROUP)

    def bd(x):
        xb = x.astype(BF16)
        return jnp.concatenate([xb, xb, xb, xb], axis=0) * bd_mask_b

    def stack(a, b):
        return jnp.concatenate([a.astype(BF16), b.astype(BF16)], axis=0)

    tr = lax.broadcasted_iota(jnp.int32, (C, 3 * C), 0)
    tc = lax.broadcasted_iota(jnp.int32, (C, 3 * C), 1) % C
    tri3 = (tr >= tc).astype(BF16)
    lw_all = lw_ref[...]
    cum_all = jnp.dot(tri3, jnp.concatenate(_split3(lw_all), axis=0), preferred_element_type=F32)
    tot_all = cum_all[C - 1:C]
    e_neg = jnp.exp(-cum_all)
    e_end = jnp.exp(tot_all - cum_all)
    kk_all = kk_ref[...]
    b_all = kk_all * lr_ref[...]
    k_all = k_ref[...]
    v_all = v_ref[...]
    at_all = -kk_all * jnp.exp(cum_all - lw_all)
    rt_all = r_ref[...] * jnp.exp(cum_all)
    bt_all = b_all * e_neg
    kt_all = k_all * e_neg
    be_all = (b_all * e_end).astype(BF16)
    ke_all = (k_all * e_end).astype(BF16)
    gamma_all = jnp.exp(tot_all)

    sl = [slice(gi * GROUP, (gi + 1) * GROUP) for gi in G]
    ar = [stack(at_all[:, s], rt_all[:, s]) for s in sl]
    p_b = [_bdot(ar[g], bd(bt_all[:, sl[g]]), NT) for g in G]
    p_k = [_bdot(ar[g], bd(kt_all[:, sl[g]]), NT) for g in G]
    l_ab = [p_b[g][:C] * low_strict for g in G]
    m_rb = [(p_b[g][C:] * low_incl).astype(BF16) for g in G]
    l_ak = [p_k[g][:C] * low_strict for g in G]
    m_rk = [p_k[g][C:] * low_incl for g in G]

    t_m = [eye_sbs + l_ab[g] for g in G]
    p = [_bdot(l_ab[g], bd(l_ab[g])) for g in G]
    for _ in range(4):
        both = [_bdot(stack(p[g], t_m[g]), bd(p[g])) for g in G]
        p = [both[g][:C] for g in G]
        t_m = [t_m[g] + both[g][C:] for g in G]
    t_m = [(t_m[g] + _bdot(t_m[g], bd(p[g]))).astype(BF16) for g in G]

    bd_v = [bd(v_all[:, s]) for s in sl]
    lkv = [_bdot(stack(l_ak[g], m_rk[g]), bd_v[g]) for g in G]
    a_h = [_bdot(t_m[g], bd(at_all[:, sl[g]])).astype(BF16) for g in G]
    u0 = [_bdot(t_m[g], bd(lkv[g][:C])).astype(BF16) for g in G]
    r_h = [rt_all[:, sl[g]] + _bdot(m_rb[g], bd(a_h[g])) for g in G]
    y0 = [_bdot(m_rb[g], bd(u0[g])) + lkv[g][C:] for g in G]
    phi = [(eye_bd * gamma_all[:, sl[g]] + _bdot(be_all[:, sl[g]], a_h[g], TN)) * bd_mask for g in G]
    psi = [(_bdot(be_all[:, sl[g]], u0[g], TN) + _bdot(ke_all[:, sl[g]], v_all[:, sl[g]], TN)) * bd_mask
           for g in G]
    both = [_bdot(stack(r_h[g], phi[g]), h_sc[g]) for g in G]
    ys = [both[g][:C] + y0[g] for g in G]
    for g in G:
        h_sc[g] = both[g][C:] + psi[g]

    y = jnp.concatenate(ys, axis=1)
    ones_pair = _ones_pair()
    m = _head_sum(y, ones_pair) * (1.0 / HEAD)
    d = y - m
    var = _head_sum(d * d, ones_pair) * (1.0 / HEAD)
    yn = d * lax.rsqrt(var + GN_EPS) * lg_ref[...] + lb_ref[...]
    bonus = _head_sum(r_ref[...] * k_ref[...] * rk_ref[...], ones_pair) * v_ref[...]
    o_ref[...] = ((yn + bonus) * g_ref[...]).astype(o_ref.dtype)


def _scan(r, lw, k, v, kkn, lr, g, r_k, lnx_g, lnx_b):
    M = r.shape[0]
    row = pl.BlockSpec((CHUNK, RW), lambda i: (i, 0))
    vec = pl.BlockSpec((1, RW), lambda i: (0, 0))
    kern = functools.partial(_scan_kernel, chunks_per_seq=SEQ // CHUNK)
    return pl.pallas_call(
        kern,
        grid=(M // CHUNK,),
        in_specs=[row] * 7 + [vec] * 3,
        out_specs=row,
        out_shape=jax.ShapeDtypeStruct((M, RW), BF16),
        scratch_shapes=[pltpu.VMEM((RW // GROUP, GROUP, GROUP), F32)],
        compiler_params=_cparams("arbitrary"),
        name="rwkv_scan",
    )(r, lw, k, v, kkn, lr, g, r_k, lnx_g, lnx_b)


def _fox_gate_kernel(fl_ref, bf_ref, c_ref, carry, *, blocks_per_seq):
    @pl.when(pl.program_id(0) % blocks_per_seq == 0)
    def _():
        carry[...] = jnp.zeros_like(carry)

    bm = fl_ref.shape[0]
    lf = jax.nn.log_sigmoid(fl_ref[...] + bf_ref[...])
    tr = lax.broadcasted_iota(jnp.int32, (bm, 3 * bm), 0)
    tc = lax.broadcasted_iota(jnp.int32, (bm, 3 * bm), 1) % bm
    tri3 = (tr >= tc).astype(BF16)
    c = jnp.dot(tri3, jnp.concatenate(_split3(lf), axis=0), preferred_element_type=F32) + carry[...]
    c_ref[...] = c
    carry[...] = c[bm - 1:bm]


def _fox_gate(fl, b_f, bm=512):
    M = fl.shape[0]
    kern = functools.partial(_fox_gate_kernel, blocks_per_seq=SEQ // bm)
    return pl.pallas_call(
        kern,
        grid=(M // bm,),
        in_specs=[pl.BlockSpec((bm, LANES), lambda m: (m, 0)),
                  pl.BlockSpec((1, LANES), lambda m: (0, 0))],
        out_specs=pl.BlockSpec((bm, LANES), lambda m: (m, 0)),
        out_shape=jax.ShapeDtypeStruct((M, LANES), F32),
        scratch_shapes=[pltpu.VMEM((1, LANES), F32)],
        compiler_params=_cparams("arbitrary"),
        name="fox_gate",
    )(fl, b_f)


def _flash_kernel(q_ref, k_ref, v_ref, c_ref, o_ref, *, blk, heads):
    qi = pl.program_id(2)
    nt = blk // LANES

    def blocks(js, state):
        new = []
        for h in range(heads):
            m_old, l_old, acc = state[h]
            hs = slice(h * FOX_HEAD, (h + 1) * FOX_HEAD)
            q = q_ref[0, :, hs]
            s2s, vs = [], []
            for j, masked in js:
                rows = pl.ds(pl.multiple_of(j * blk, blk), blk)
                ck2 = c_ref[0, h, pl.ds(j, 1), :] * LOG2E
                s2 = lax.dot_general(q, k_ref[0, rows, hs], NT, preferred_element_type=F32) - ck2
                if masked:
                    row = lax.broadcasted_iota(jnp.int32, (blk, blk), 0)
                    col = lax.broadcasted_iota(jnp.int32, (blk, blk), 1)
                    s2 = jnp.where(row >= col, s2, -jnp.inf)
                s2s.append(s2)
                vs.append(v_ref[0, rows, hs])
            m_new = m_old
            for s2 in s2s:
                m_new = jnp.maximum(m_new, jnp.max(s2, axis=-1, keepdims=True))
            alpha = jnp.exp2(m_old - m_new)
            m_wide = jnp.concatenate([m_new] * nt, axis=1)
            l_new = alpha * l_old
            acc = alpha * acc
            for s2, v in zip(s2s, vs):
                p = jnp.exp2(s2 - m_wide)
                l_new = l_new + jnp.sum(p, axis=-1, keepdims=True)
                acc = acc + jnp.dot(p.astype(BF16), v, preferred_element_type=F32)
            new.append((m_new, l_new, acc))
        return tuple(new)

    init = tuple((jnp.full((blk, LANES), -jnp.inf, F32), jnp.zeros((blk, LANES), F32),
                  jnp.zeros((blk, FOX_HEAD), F32)) for _ in range(heads))
    state = lax.fori_loop(0, qi // 2,
                          lambda i, st: blocks([(2 * i, False), (2 * i + 1, False)], st), init)
    state = lax.cond(qi % 2 == 1,
                     lambda st: blocks([(qi - 1, False), (qi, True)], st),
                     lambda st: blocks([(qi, True)], st), state)
    for h in range(heads):
        _, l_fin, acc = state[h]
        o_ref[0, :, h * FOX_HEAD:(h + 1) * FOX_HEAD] = (acc / l_fin).astype(o_ref.dtype)


def _flash(qkv, c_blk, blk=512, heads=4):
    B, T, _ = qkv.shape
    hg = FOX_HEADS // heads
    w = heads * FOX_HEAD
    kern = functools.partial(_flash_kernel, blk=blk, heads=heads)
    return pl.pallas_call(
        kern,
        grid=(B, hg, T // blk),
        in_specs=[pl.BlockSpec((1, blk, w), lambda b, g, qi: (b, qi, g)),
                  pl.BlockSpec((1, T, w), lambda b, g, qi: (b, 0, hg + g)),
                  pl.BlockSpec((1, T, w), lambda b, g, qi: (b, 0, 2 * hg + g)),
                  pl.BlockSpec((1, heads, T // blk, blk), lambda b, g, qi: (b, g, 0, 0))],
        out_specs=pl.BlockSpec((1, blk, w), lambda b, g, qi: (b, qi, g)),
        out_shape=jax.ShapeDtypeStruct((B, T, D_MODEL), BF16),
        compiler_params=_cparams("parallel", "parallel", "arbitrary"),
        name="fox_flash",
    )(qkv, qkv, qkv, c_blk)


def _pad_cols(w, width):
    return jnp.pad(w, ((0, 0), (0, width - w.shape[1])))


def _pad_rows(w, height):
    return jnp.pad(w, ((0, height - w.shape[0]), (0, 0)))


def _even_layer(xf, xb, w_in, mu, w0, w2, a0, a2, g2, k_k, k_a, r_k, lnx_g, lnx_b,
                conv_w, w_out, v_first, vres, ln_g, ln_b):
    rw3 = 3 * RW
    o_dw, o_da, o_dg = rw3, rw3 + DECAY_LORA, rw3 + DECAY_LORA + AAA_LORA
    n_rwkv = o_dg + GATE_LORA

    def lay(a):
        return jnp.concatenate([a[..., :rw3],
                                _pad_cols(a[..., o_dw:o_da], LANES),
                                _pad_cols(a[..., o_da:o_dg], LANES),
                                _pad_cols(a[..., o_dg:n_rwkv], 2 * LANES)], axis=-1)

    pr = _mm(xb, lay(w_in[:, :n_rwkv]), F32, bm=512, bn=RWKV_PAD // 2)
    y_conv = _sconv(xb, w_in[:, n_rwkv:], conv_w)
    vec = lambda a: a.reshape(1, -1)
    if vres is not None:
        v0, v1, v2 = vres
        vres_args = (vec(v0), _pad_cols(v1, LANES).astype(BF16), _pad_rows(v2, LANES).astype(BF16), v_first)
    else:
        vres_args = None
    r, lw, k, v, kkn, lr, g = _rwkv_prep(
        pr, lay(vec(mu)), vec(w0), _pad_rows(w2, LANES).astype(BF16), vec(a0),
        _pad_rows(a2, LANES).astype(BF16), _pad_rows(g2, 2 * LANES).astype(BF16),
        vec(k_k), vec(k_a), vres_args)
    if vres is None:
        v_first = v
    y_rwkv = _scan(r, lw, k, v, kkn, lr, g, vec(r_k), vec(lnx_g), vec(lnx_b))
    xf, xb = _mm_ln([y_rwkv, y_conv], w_out.astype(BF16), xf, ln_g, ln_b)
    return xf, xb, v_first


def _odd_layer(xf, xb, w_in, layer, b_f, w_out, ln_g, ln_b):
    M = xb.shape[0]
    B = M // SEQ
    d3 = 3 * D_MODEL
    k_scale = jnp.concatenate([jnp.ones((1, D_MODEL), F32),
                               jnp.full((1, D_MODEL), FOX_SCALE_LOG2, F32),
                               jnp.ones((1, D_MODEL), F32)], axis=1)
    qkv = _mm(xb, w_in, BF16, bm=1024, bn=1024, n_cols=d3, col_scale=k_scale, layer=layer)
    fl = _mm(xb, _pad_cols(w_in[layer, :, d3:], LANES), F32, bm=1024, bn=LANES)
    c = _fox_gate(fl, _pad_cols(b_f.reshape(1, -1), LANES))
    c = c.reshape(B, SEQ, LANES)[:, :, :FOX_HEADS]
    blk = 512
    c_t = jnp.swapaxes(c, 1, 2).reshape(B, FOX_HEADS, SEQ // blk, blk)
    o = _flash(qkv.reshape(B, SEQ, d3), c_t, blk=blk)
    return _mm_ln([o.reshape(M, D_MODEL)], w_out.astype(BF16), xf, ln_g, ln_b)


def _conv_ffn(xf, xb, w_up, conv_w, conv_b, w_down, layer, ln_g, ln_b):
    h = _ffn_up(xb, w_up, conv_w, conv_b.reshape(conv_b.shape[0], 1, -1), layer)
    return _mm_ln([h], w_down, xf, ln_g, ln_b, bm=256, layer=layer)


def kernel(x, ln_g, ln_b, ev_w_in, ev_mu, ev_w0, ev_w2, ev_a0, ev_a2, ev_g2, ev_k_k, ev_k_a, ev_r_k, ev_lnx_g, ev_lnx_b, ev_v0, ev_v1, ev_v2, ev_conv_w, ev_w_out, od_w_in, od_b_f, od_w_out, ff_w_up, ff_conv_w, ff_conv_b, ff_w_down):
    B, T, D = x.shape
    xf = x.reshape(B * T, D)
    xb = xf.astype(BF16)
    w_down_b = ff_w_down.astype(BF16)
    v_first = None
    for i in range(DEPTH):
        if i % 2 == 0:
            e = i // 2
            vres = (ev_v0[e - 1], ev_v1[e - 1], ev_v2[e - 1]) if e > 0 else None
            xf, xb, v_first = _even_layer(xf, xb, ev_w_in[e], ev_mu[e], ev_w0[e], ev_w2[e], ev_a0[e],
                                          ev_a2[e], ev_g2[e], ev_k_k[e], ev_k_a[e], ev_r_k[e],
                                          ev_lnx_g[e], ev_lnx_b[e], ev_conv_w[e], ev_w_out[e],
                                          v_first, vres, ln_g[i, 0], ln_b[i, 0])
        else:
            o = i // 2
            xf, xb = _odd_layer(xf, xb, od_w_in, o, od_b_f[o], od_w_out[o], ln_g[i, 0], ln_b[i, 0])
        xf, xb = _conv_ffn(xf, xb, ff_w_up, ff_conv_w, ff_conv_b, w_down_b, i,
                           ln_g[i, 1], ln_b[i, 1])
    return xf.reshape(B, T, D)
```

```python
import functools

import jax
import jax.numpy as jnp
from jax import lax
from jax.experimental import pallas as pl
from jax.experimental.pallas import tpu as pltpu

F32 = jnp.float32
BF16 = jnp.bfloat16

D_MODEL = 2048
SEQ = 4096
DEPTH = 4
RW = 1024
HEAD = 64
DECAY_LORA = 64
AAA_LORA = 64
MV_LORA = 32
GATE_LORA = 160
CW = 1024
FOX_HEADS = 16
FOX_HEAD = 128
D_FF = 5632
GN_EPS = 64e-5
LN_EPS = 1e-5
DN_ALPHA = (2 * DEPTH) ** 0.25
LOG2E = 1.4426950408889634
FOX_SCALE_LOG2 = FOX_HEAD ** -0.5 * LOG2E

LANES = 128
SUBLANES = 8
VMEM_LIMIT = 56 * 1024 * 1024

LORA_W = 3 * RW
LORA_A = LORA_W + LANES
LORA_G = LORA_A + LANES
RWKV_PAD = LORA_G + 2 * LANES

CHUNK = 64
GROUP = 4 * HEAD

NN = (((1,), (0,)), ((), ()))
NT = (((1,), (1,)), ((), ()))
TN = (((0,), (0,)), ((), ()))


def _cparams(*sem):
    return pltpu.CompilerParams(dimension_semantics=sem, vmem_limit_bytes=VMEM_LIMIT)


def _bdot(a, b, dims=NN):
    return lax.dot_general(a.astype(BF16), b.astype(BF16), dims, preferred_element_type=F32)


def _split3(x):
    hi = x.astype(BF16)
    r1 = x - hi.astype(F32)
    mid = r1.astype(BF16)
    lo = (r1 - mid.astype(F32)).astype(BF16)
    return hi, mid, lo


def _shift_rows(u, prev8, s):
    ext = jnp.concatenate([prev8, u], axis=0)
    return pltpu.roll(ext, s, axis=0)[SUBLANES:]


def _head_sum(x, ones_pair):
    outs = []
    for t in range(x.shape[1] // LANES):
        xt = x[:, t * LANES:(t + 1) * LANES]
        hi = xt.astype(BF16)
        lo = (xt - hi.astype(F32)).astype(BF16)
        outs.append(jnp.dot(jnp.concatenate([hi, lo], axis=1), ones_pair,
                            preferred_element_type=F32))
    return jnp.concatenate(outs, axis=1)


def _ones_pair():
    r = lax.broadcasted_iota(jnp.int32, (2 * LANES, LANES), 0)
    c = lax.broadcasted_iota(jnp.int32, (2 * LANES, LANES), 1)
    return jnp.where((r % LANES) // HEAD == c // HEAD, 1.0, 0.0).astype(BF16)


def _mm_kernel(x_ref, w_ref, s_ref, o_ref, wb):
    @pl.when(pl.program_id(1) == 0)
    def _():
        wb[...] = w_ref[...].astype(BF16)

    acc = jnp.dot(x_ref[...], wb[...], preferred_element_type=F32)
    o_ref[...] = (acc * s_ref[...]).astype(o_ref.dtype)


def _layer_spec(w, layer, block, index):
    if w.ndim == 2:
        return pl.BlockSpec(block, index)
    return pl.BlockSpec((None,) + block, lambda n, m: (layer,) + index(n, m))


def _mm(x, w, out_dtype, bm, bn, n_cols=None, col_scale=None, layer=None):
    M, K = x.shape
    N = w.shape[-1] if n_cols is None else n_cols
    if col_scale is None:
        col_scale = jnp.ones((1, N), F32)
    return pl.pallas_call(
        _mm_kernel,
        grid=(N // bn, M // bm),
        in_specs=[pl.BlockSpec((bm, K), lambda n, m: (m, 0)),
                  _layer_spec(w, layer, (K, bn), lambda n, m: (0, n)),
                  pl.BlockSpec((1, bn), lambda n, m: (0, n))],
        out_specs=pl.BlockSpec((bm, bn), lambda n, m: (m, n)),
        out_shape=jax.ShapeDtypeStruct((M, N), out_dtype),
        scratch_shapes=[pltpu.VMEM((K, bn), BF16)],
        compiler_params=_cparams("parallel", "arbitrary"),
        name="mm",
    )(x, w, col_scale)


def _deepnorm(x, y, g, b):
    z = DN_ALPHA * x + y
    mu = jnp.mean(z, axis=-1, keepdims=True)
    d = z - mu
    var = jnp.mean(d * d, axis=-1, keepdims=True)
    return d * lax.rsqrt(var + LN_EPS) * g + b


def _mm_ln_kernel(*refs, n_x):
    x_refs = refs[:n_x]
    w_ref, r_ref, g_ref, b_ref, o_ref, ob_ref = refs[n_x:]
    x = x_refs[0][...] if n_x == 1 else jnp.concatenate([xr[...] for xr in x_refs], axis=1)
    y = jnp.dot(x, w_ref[...], preferred_element_type=F32)
    o = _deepnorm(r_ref[...], y, g_ref[...], b_ref[...])
    o_ref[...] = o
    ob_ref[...] = o.astype(BF16)


def _mm_ln(xs, w, resid, g, b, bm=512, layer=None):
    M, D = resid.shape
    row = pl.BlockSpec((bm, D), lambda m: (m, 0))
    vec = pl.BlockSpec((1, D), lambda m: (0, 0))
    if w.ndim == 2:
        w_spec = pl.BlockSpec(w.shape, lambda m: (0, 0), pipeline_mode=pl.Buffered(1))
    else:
        w_spec = pl.BlockSpec((None,) + w.shape[1:], lambda m: (layer, 0, 0),
                              pipeline_mode=pl.Buffered(1))
    kern = functools.partial(_mm_ln_kernel, n_x=len(xs))
    return pl.pallas_call(
        kern,
        grid=(M // bm,),
        in_specs=[pl.BlockSpec((bm, x.shape[1]), lambda m: (m, 0)) for x in xs]
        + [w_spec, row, vec, vec],
        out_specs=[row, row],
        out_shape=[jax.ShapeDtypeStruct((M, D), F32), jax.ShapeDtypeStruct((M, D), BF16)],
        compiler_params=_cparams("parallel"),
        name="mm_ln",
    )(*xs, w, resid, g.reshape(1, D), b.reshape(1, D))


def _ln_res_kernel(x_ref, y_ref, g_ref, b_ref, o_ref, ob_ref):
    o = _deepnorm(x_ref[...], y_ref[...], g_ref[...], b_ref[...])
    o_ref[...] = o
    ob_ref[...] = o.astype(BF16)


def _ln_res(x, y, g, b, bm=512):
    M, D = x.shape
    row = pl.BlockSpec((bm, D), lambda m: (m, 0))
    vec = pl.BlockSpec((1, D), lambda m: (0, 0))
    return pl.pallas_call(
        _ln_res_kernel,
        grid=(M // bm,),
        in_specs=[row, row, vec, vec],
        out_specs=[row, row],
        out_shape=[jax.ShapeDtypeStruct((M, D), F32), jax.ShapeDtypeStruct((M, D), BF16)],
        compiler_params=_cparams("parallel"),
        name="ln_res",
    )(x, y, g.reshape(1, D), b.reshape(1, D))


def _ffn_up_kernel(x_ref, wg_ref, wv_ref, cwg_ref, cwv_ref, cbg_ref, cbv_ref, h_ref,
                   wgb, wvb, carry_g, carry_v, *, blocks_per_seq):
    m = pl.program_id(1)
    bm = x_ref.shape[0]
    sub = 2 * LANES

    @pl.when(m == 0)
    def _():
        wgb[...] = wg_ref[...].astype(BF16)
        wvb[...] = wv_ref[...].astype(BF16)

    first = m % blocks_per_seq == 0
    x = x_ref[...]

    def branch(w_sc, cw_ref, cb_ref, carry, cs):
        u = jnp.dot(x, w_sc[:, cs], preferred_element_type=F32)
        prev = jnp.where(first, 0.0, carry[:, cs])
        cw = cw_ref[:, cs]
        y = (_shift_rows(u, prev, 2) * cw[0:1] + _shift_rows(u, prev, 1) * cw[1:2]
             + u * cw[2:3] + cb_ref[:, cs])
        carry[:, cs] = u[bm - SUBLANES:]
        return y

    for c in range(h_ref.shape[1] // sub):
        cs = slice(c * sub, (c + 1) * sub)
        gate = branch(wgb, cwg_ref, cbg_ref, carry_g, cs)
        val = branch(wvb, cwv_ref, cbv_ref, carry_v, cs)
        gb = gate.astype(BF16)
        h_ref[:, cs] = gb * jax.nn.sigmoid(gb) * val.astype(BF16)


def _ffn_up(x, w_up, conv_w, conv_b, layer, bm=1024, bn=512):
    M, K = x.shape
    nb = D_FF // bn
    kern = functools.partial(_ffn_up_kernel, blocks_per_seq=SEQ // bm)
    lo = lambda n, m: (0, n)
    hi = lambda n, m: (0, n + nb)
    return pl.pallas_call(
        kern,
        grid=(nb, M // bm),
        in_specs=[pl.BlockSpec((bm, K), lambda n, m: (m, 0)),
                  _layer_spec(w_up, layer, (K, bn), lo),
                  _layer_spec(w_up, layer, (K, bn), hi),
                  _layer_spec(conv_w, layer, (3, bn), lo),
                  _layer_spec(conv_w, layer, (3, bn), hi),
                  _layer_spec(conv_b, layer, (1, bn), lo),
                  _layer_spec(conv_b, layer, (1, bn), hi)],
        out_specs=pl.BlockSpec((bm, bn), lambda n, m: (m, n)),
        out_shape=jax.ShapeDtypeStruct((M, D_FF), BF16),
        scratch_shapes=[pltpu.VMEM((K, bn), BF16), pltpu.VMEM((K, bn), BF16),
                        pltpu.VMEM((SUBLANES, bn), F32), pltpu.VMEM((SUBLANES, bn), F32)],
        compiler_params=_cparams("parallel", "arbitrary"),
        name="ffn_up",
    )(x, w_up, w_up, conv_w, conv_w, conv_b, conv_b)


def _sconv_kernel(x_ref, wb_ref, wc_ref, wh_ref, cw_ref, o_ref, wbb, wcb, whb, carry,
                  *, blocks_per_seq):
    m = pl.program_id(1)

    @pl.when(m == 0)
    def _():
        wbb[...] = wb_ref[...].astype(BF16)
        wcb[...] = wc_ref[...].astype(BF16)
        whb[...] = wh_ref[...].astype(BF16)

    x = x_ref[...]
    gb = jnp.dot(x, wbb[...], preferred_element_type=F32)
    gc = jnp.dot(x, wcb[...], preferred_element_type=F32)
    h = jnp.dot(x, whb[...], preferred_element_type=F32)
    u = gc * h
    prev = jnp.where(m % blocks_per_seq == 0, 0.0, carry[...])
    cw = cw_ref[...]
    y = _shift_rows(u, prev, 2) * cw[0:1] + _shift_rows(u, prev, 1) * cw[1:2] + u * cw[2:3]
    carry[...] = u[u.shape[0] - SUBLANES:]
    o_ref[...] = (gb * y).astype(o_ref.dtype)


def _sconv(x, w_conv, conv_w, bm=1024, bn=256):
    M, K = x.shape
    nb = CW // bn
    kern = functools.partial(_sconv_kernel, blocks_per_seq=SEQ // bm)
    return pl.pallas_call(
        kern,
        grid=(nb, M // bm),
        in_specs=[pl.BlockSpec((bm, K), lambda n, m: (m, 0)),
                  pl.BlockSpec((K, bn), lambda n, m: (0, n)),
                  pl.BlockSpec((K, bn), lambda n, m: (0, n + nb)),
                  pl.BlockSpec((K, bn), lambda n, m: (0, n + 2 * nb)),
                  pl.BlockSpec((3, bn), lambda n, m: (0, n))],
        out_specs=pl.BlockSpec((bm, bn), lambda n, m: (m, n)),
        out_shape=jax.ShapeDtypeStruct((M, CW), BF16),
        scratch_shapes=[pltpu.VMEM((K, bn), BF16)] * 3 + [pltpu.VMEM((SUBLANES, bn), F32)],
        compiler_params=_cparams("parallel", "arbitrary"),
        name="sconv",
    )(x, w_conv, w_conv, w_conv, conv_w)


def _rwkv_prep_kernel(*refs, has_vres, blocks_per_seq):
    if has_vres:
        (pr_ref, prev_ref, mu_ref, w0_ref, w2_ref, a0_ref, a2_ref, g2_ref, kk_ref, ka_ref,
         v0_ref, v1_ref, v2_ref, vf_ref,
         r_o, lw_o, k_o, v_o, kkn_o, lr_o, g_o) = refs
    else:
        (pr_ref, prev_ref, mu_ref, w0_ref, w2_ref, a0_ref, a2_ref, g2_ref, kk_ref, ka_ref,
         r_o, lw_o, k_o, v_o, kkn_o, lr_o, g_o) = refs
    first = pl.program_id(0) % blocks_per_seq == 0
    pr = pr_ref[...]
    prev = jnp.where(first, 0.0, prev_ref[...])
    x = pr + (_shift_rows(pr, prev, 1) - pr) * mu_ref[...]
    r = x[:, 0:RW]
    k = x[:, RW:2 * RW]
    v = x[:, 2 * RW:3 * RW]
    dw = x[:, LORA_W:LORA_W + LANES]
    da = x[:, LORA_A:LORA_A + LANES]
    dg = x[:, LORA_G:LORA_G + 2 * LANES]
    z = w0_ref[...] + _bdot(jnp.tanh(dw), w2_ref[...])
    lw = -jnp.exp(F32(-0.5)) * jax.nn.sigmoid(z)
    lr = jax.nn.sigmoid(a0_ref[...] + _bdot(da, a2_ref[...]))
    g = _bdot(jax.nn.sigmoid(dg), g2_ref[...])
    if has_vres:
        mix = jax.nn.sigmoid(v0_ref[...] + _bdot(_bdot(v, v1_ref[...]), v2_ref[...]))
        v = v + (vf_ref[...] - v) * mix
    kkr = k * kk_ref[...]
    ss = _head_sum(kkr * kkr, _ones_pair())
    kkn = kkr / jnp.maximum(jnp.sqrt(ss), 1e-12)
    k2 = k * (1.0 + (lr - 1.0) * ka_ref[...])
    r_o[...] = r
    lw_o[...] = lw
    k_o[...] = k2
    v_o[...] = v
    kkn_o[...] = kkn
    lr_o[...] = lr
    g_o[...] = g


def _rwkv_prep(pr, mu, w0, w2, a0, a2, g2, k_k, k_a, vres, bm=256):
    M = pr.shape[0]
    has_vres = vres is not None
    row = lambda w: pl.BlockSpec((bm, w), lambda m: (m, 0))
    full = lambda a: pl.BlockSpec(a.shape, lambda m: (0, 0))
    per8 = bm // SUBLANES
    args = [pr, pr, mu, w0, w2, a0, a2, g2, k_k, k_a]
    in_specs = [row(RWKV_PAD),
                pl.BlockSpec((SUBLANES, RWKV_PAD), lambda m: (jnp.maximum(m * per8 - 1, 0), 0)),
                full(mu), full(w0), full(w2), full(a0), full(a2), full(g2), full(k_k), full(k_a)]
    if has_vres:
        v0, v1, v2, vf = vres
        args += [v0, v1, v2, vf]
        in_specs += [full(v0), full(v1), full(v2), row(RW)]
    kern = functools.partial(_rwkv_prep_kernel, has_vres=has_vres, blocks_per_seq=SEQ // bm)
    return pl.pallas_call(
        kern,
        grid=(M // bm,),
        in_specs=in_specs,
        out_specs=[row(RW)] * 7,
        out_shape=[jax.ShapeDtypeStruct((M, RW), F32)] * 7,
        compiler_params=_cparams("parallel"),
        name="rwkv_prep",
    )(*args)


def _scan_kernel(r_ref, lw_ref, k_ref, v_ref, kk_ref, lr_ref, g_ref, rk_ref, lg_ref, lb_ref,
                 o_ref, h_sc, *, chunks_per_seq):
    C = CHUNK

    @pl.when(pl.program_id(0) % chunks_per_seq == 0)
    def _():
        h_sc[...] = jnp.zeros_like(h_sc)

    row = lax.broadcasted_iota(jnp.int32, (C, GROUP), 0)
    col = lax.broadcasted_iota(jnp.int32, (C, GROUP), 1) % HEAD
    low_strict = (row > col).astype(F32)
    low_incl = (row >= col).astype(F32)
    eye_sbs = (row == col).astype(F32)
    br = lax.broadcasted_iota(jnp.int32, (GROUP, GROUP), 0)
    bc = lax.broadcasted_iota(jnp.int32, (GROUP, GROUP), 1)
    bd_mask = (br // HEAD == bc // HEAD).astype(F32)
    bd_mask_b = bd_mask.astype(BF16)
    eye_bd = (br == bc).astype(F32)
    G = range(RW // GROUP)

    def bd(x):
        xb = x.astype(BF16)
        return jnp.concatenate([xb, xb, xb, xb], axis=0) * bd_mask_b

    def stack(a, b):
        return jnp.concatenate([a.astype(BF16), b.astype(BF16)], axis=0)

    tr = lax.broadcasted_iota(jnp.int32, (C, 3 * C), 0)
    tc = lax.broadcasted_iota(jnp.int32, (C, 3 * C), 1) % C
    tri3 = (tr >= tc).astype(BF16)
    lw_all = lw_ref[...]
    cum_all = jnp.dot(tri3, jnp.concatenate(_split3(lw_all), axis=0), preferred_element_type=F32)
    tot_all = cum_all[C - 1:C]
    e_neg = jnp.exp(-cum_all)
    e_end = jnp.exp(tot_all - cum_all)
    kk_all = kk_ref[...]
    b_all = kk_all * lr_ref[...]
    k_all = k_ref[...]
    v_all = v_ref[...]
    at_all = -kk_all * jnp.exp(cum_all - lw_all)
    rt_all = r_ref[...] * jnp.exp(cum_all)
    bt_all = b_all * e_neg
    kt_all = k_all * e_neg
    be_all = (b_all * e_end).astype(BF16)
    ke_all = (k_all * e_end).astype(BF16)
    gamma_all = jnp.exp(tot_all)

    sl = [slice(gi * GROUP, (gi + 1) * GROUP) for gi in G]
    ar = [stack(at_all[:, s], rt_all[:, s]) for s in sl]
    p_b = [_bdot(ar[g], bd(bt_all[:, sl[g]]), NT) for g in G]
    p_k = [_bdot(ar[g], bd(kt_all[:, sl[g]]), NT) for g in G]
    l_ab = [p_b[g][:C] * low_strict for g in G]
    m_rb = [(p_b[g][C:] * low_incl).astype(BF16) for g in G]
    l_ak = [p_k[g][:C] * low_strict for g in G]
    m_rk = [p_k[g][C:] * low_incl for g in G]

    t_m = [eye_sbs + l_ab[g] for g in G]
    p = [_bdot(l_ab[g], bd(l_ab[g])) for g in G]
    for _ in range(4):
        both = [_bdot(stack(p[g], t_m[g]), bd(p[g])) for g in G]
        p = [both[g][:C] for g in G]
        t_m = [t_m[g] + both[g][C:] for g in G]
    t_m = [(t_m[g] + _bdot(t_m[g], bd(p[g]))).astype(BF16) for g in G]

    bd_v = [bd(v_all[:, s]) for s in sl]
    lkv = [_bdot(stack(l_ak[g], m_rk[g]), bd_v[g]) for g in G]
    a_h = [_bdot(t_m[g], bd(at_all[:, sl[g]])).astype(BF16) for g in G]
    u0 = [_bdot(t_m[g], bd(lkv[g][:C])).astype(BF16) for g in G]
    r_h = [rt_all[:, sl[g]] + _bdot(m_rb[g], bd(a_h[g])) for g in G]
    y0 = [_bdot(m_rb[g], bd(u0[g])) + lkv[g][C:] for g in G]
    phi = [(eye_bd * gamma_all[:, sl[g]] + _bdot(be_all[:, sl[g]], a_h[g], TN)) * bd_mask for g in G]
    psi = [(_bdot(be_all[:, sl[g]], u0[g], TN) + _bdot(ke_all[:, sl[g]], v_all[:, sl[g]], TN)) * bd_mask
           for g in G]
    both = [_bdot(stack(r_h[g], phi[g]), h_sc[g]) for g in G]
    ys = [both[g][:C] + y0[g] for g in G]
    for g in G:
        h_sc[g] = both[g][C:] + psi[g]

    y = jnp.concatenate(ys, axis=1)
    ones_pair = _ones_pair()
    m = _head_sum(y, ones_pair) * (1.0 / HEAD)
    d = y - m
    var = _head_sum(d * d, ones_pair) * (1.0 / HEAD)
    yn = d * lax.rsqrt(var + GN_EPS) * lg_ref[...] + lb_ref[...]
    bonus = _head_sum(r_ref[...] * k_ref[...] * rk_ref[...], ones_pair) * v_ref[...]
    o_ref[...] = ((yn + bonus) * g_ref[...]).astype(o_ref.dtype)


def _scan(r, lw, k, v, kkn, lr, g, r_k, lnx_g, lnx_b):
    M = r.shape[0]
    row = pl.BlockSpec((CHUNK, RW), lambda i: (i, 0))
    vec = pl.BlockSpec((1, RW), lambda i: (0, 0))
    kern = functools.partial(_scan_kernel, chunks_per_seq=SEQ // CHUNK)
    return pl.pallas_call(
        kern,
        grid=(M // CHUNK,),
        in_specs=[row] * 7 + [vec] * 3,
        out_specs=row,
        out_shape=jax.ShapeDtypeStruct((M, RW), BF16),
        scratch_shapes=[pltpu.VMEM((RW // GROUP, GROUP, GROUP), F32)],
        compiler_params=_cparams("arbitrary"),
        name="rwkv_scan",
    )(r, lw, k, v, kkn, lr, g, r_k, lnx_g, lnx_b)


def _fox_gate_kernel(fl_ref, bf_ref, c_ref, carry, *, blocks_per_seq):
    @pl.when(pl.program_id(0) % blocks_per_seq == 0)
    def _():
        carry[...] = jnp.zeros_like(carry)

    bm = fl_ref.shape[0]
    lf = jax.nn.log_sigmoid(fl_ref[...] + bf_ref[...])
    tr = lax.broadcasted_iota(jnp.int32, (bm, 3 * bm), 0)
    tc = lax.broadcasted_iota(jnp.int32, (bm, 3 * bm), 1) % bm
    tri3 = (tr >= tc).astype(BF16)
    c = jnp.dot(tri3, jnp.concatenate(_split3(lf), axis=0), preferred_element_type=F32) + carry[...]
    c_ref[...] = c
    carry[...] = c[bm - 1:bm]


def _fox_gate(fl, b_f, bm=512):
    M = fl.shape[0]
    kern = functools.partial(_fox_gate_kernel, blocks_per_seq=SEQ // bm)
    return pl.pallas_call(
        kern,
        grid=(M // bm,),
        in_specs=[pl.BlockSpec((bm, LANES), lambda m: (m, 0)),
                  pl.BlockSpec((1, LANES), lambda m: (0, 0))],
        out_specs=pl.BlockSpec((bm, LANES), lambda m: (m, 0)),
        out_shape=jax.ShapeDtypeStruct((M, LANES), F32),
        scratch_shapes=[pltpu.VMEM((1, LANES), F32)],
        compiler_params=_cparams("arbitrary"),
        name="fox_gate",
    )(fl, b_f)


def _flash_kernel(q_ref, k_ref, v_ref, c_ref, o_ref, *, blk, heads):
    qi = pl.program_id(2)
    nt = blk // LANES

    def blocks(js, state):
        new = []
        for h in range(heads):
            m_old, l_old, acc = state[h]
            hs = slice(h * FOX_HEAD, (h + 1) * FOX_HEAD)
            q = q_ref[0, :, hs]
            s2s, vs = [], []
            for j, masked in js:
                rows = pl.ds(pl.multiple_of(j * blk, blk), blk)
                ck2 = c_ref[0, h, pl.ds(j, 1), :] * LOG2E
                s2 = lax.dot_general(q, k_ref[0, rows, hs], NT, preferred_element_type=F32) - ck2
                if masked:
                    row = lax.broadcasted_iota(jnp.int32, (blk, blk), 0)
                    col = lax.broadcasted_iota(jnp.int32, (blk, blk), 1)
                    s2 = jnp.where(row >= col, s2, -jnp.inf)
                s2s.append(s2)
                vs.append(v_ref[0, rows, hs])
            m_new = m_old
            for s2 in s2s:
                m_new = jnp.maximum(m_new, jnp.max(s2, axis=-1, keepdims=True))
            alpha = jnp.exp2(m_old - m_new)
            m_wide = jnp.concatenate([m_new] * nt, axis=1)
            l_new = alpha * l_old
            acc = alpha * acc
            for s2, v in zip(s2s, vs):
                p = jnp.exp2(s2 - m_wide)
                l_new = l_new + jnp.sum(p, axis=-1, keepdims=True)
                acc = acc + jnp.dot(p.astype(BF16), v, preferred_element_type=F32)
            new.append((m_new, l_new, acc))
        return tuple(new)

    init = tuple((jnp.full((blk, LANES), -jnp.inf, F32), jnp.zeros((blk, LANES), F32),
                  jnp.zeros((blk, FOX_HEAD), F32)) for _ in range(heads))
    state = lax.fori_loop(0, qi // 2,
                          lambda i, st: blocks([(2 * i, False), (2 * i + 1, False)], st), init)
    state = lax.cond(qi % 2 == 1,
                     lambda st: blocks([(qi - 1, False), (qi, True)], st),
                     lambda st: blocks([(qi, True)], st), state)
    for h in range(heads):
        _, l_fin, acc = state[h]
        o_ref[0, :, h * FOX_HEAD:(h + 1) * FOX_HEAD] = (acc / l_fin).astype(o_ref.dtype)


def _flash(qkv, c_blk, blk=512, heads=4):
    B, T, _ = qkv.shape
    hg = FOX_HEADS // heads
    w = heads * FOX_HEAD
    kern = functools.partial(_flash_kernel, blk=blk, heads=heads)
    return pl.pallas_call(
        kern,
        grid=(B, hg, T // blk),
        in_specs=[pl.BlockSpec((1, blk, w), lambda b, g, qi: (b, qi, g)),
                  pl.BlockSpec((1, T, w), lambda b, g, qi: (b, 0, hg + g)),
                  pl.BlockSpec((1, T, w), lambda b, g, qi: (b, 0, 2 * hg + g)),
                  pl.BlockSpec((1, heads, T // blk, blk), lambda b, g, qi: (b, g, 0, 0))],
        out_specs=pl.BlockSpec((1, blk, w), lambda b, g, qi: (b, qi, g)),
        out_shape=jax.ShapeDtypeStruct((B, T, D_MODEL), BF16),
        compiler_params=_cparams("parallel", "parallel", "arbitrary"),
        name="fox_flash",
    )(qkv, qkv, qkv, c_blk)


def _pad_cols(w, width):
    return jnp.pad(w, ((0, 0), (0, width - w.shape[1])))


def _pad_rows(w, height):
    return jnp.pad(w, ((0, height - w.shape[0]), (0, 0)))


def _even_layer(xf, xb, w_in, mu, w0, w2, a0, a2, g2, k_k, k_a, r_k, lnx_g, lnx_b,
                conv_w, w_out, v_first, vres, ln_g, ln_b):
    rw3 = 3 * RW
    o_dw, o_da, o_dg = rw3, rw3 + DECAY_LORA, rw3 + DECAY_LORA + AAA_LORA
    n_rwkv = o_dg + GATE_LORA

    def lay(a):
        return jnp.concatenate([a[..., :rw3],
                                _pad_cols(a[..., o_dw:o_da], LANES),
                                _pad_cols(a[..., o_da:o_dg], LANES),
                                _pad_cols(a[..., o_dg:n_rwkv], 2 * LANES)], axis=-1)

    pr = _mm(xb, lay(w_in[:, :n_rwkv]), F32, bm=512, bn=RWKV_PAD // 2)
    y_conv = _sconv(xb, w_in[:, n_rwkv:], conv_w)
    vec = lambda a: a.reshape(1, -1)
    if vres is not None:
        v0, v1, v2 = vres
        vres_args = (vec(v0), _pad_cols(v1, LANES).astype(BF16), _pad_rows(v2, LANES).astype(BF16), v_first)
    else:
        vres_args = None
    r, lw, k, v, kkn, lr, g = _rwkv_prep(
        pr, lay(vec(mu)), vec(w0), _pad_rows(w2, LANES).astype(BF16), vec(a0),
        _pad_rows(a2, LANES).astype(BF16), _pad_rows(g2, 2 * LANES).astype(BF16),
        vec(k_k), vec(k_a), vres_args)
    if vres is None:
        v_first = v
    y_rwkv = _scan(r, lw, k, v, kkn, lr, g, vec(r_k), vec(lnx_g), vec(lnx_b))
    xf, xb = _mm_ln([y_rwkv, y_conv], w_out.astype(BF16), xf, ln_g, ln_b)
    return xf, xb, v_first


def _odd_layer(xf, xb, w_in, layer, b_f, w_out, ln_g, ln_b):
    M = xb.shape[0]
    B = M // SEQ
    d3 = 3 * D_MODEL
    k_scale = jnp.concatenate([jnp.ones((1, D_MODEL), F32),
                               jnp.full((1, D_MODEL), FOX_SCALE_LOG2, F32),
                               jnp.ones((1, D_MODEL), F32)], axis=1)
    qkv = _mm(xb, w_in, BF16, bm=1024, bn=1024, n_cols=d3, col_scale=k_scale, layer=layer)
    fl = _mm(xb, _pad_cols(w_in[layer, :, d3:], LANES), F32, bm=1024, bn=LANES)
    c = _fox_gate(fl, _pad_cols(b_f.reshape(1, -1), LANES))
    c = c.reshape(B, SEQ, LANES)[:, :, :FOX_HEADS]
    blk = 512
    c_t = jnp.swapaxes(c, 1, 2).reshape(B, FOX_HEADS, SEQ // blk, blk)
    o = _flash(qkv.reshape(B, SEQ, d3), c_t, blk=blk)
    return _mm_ln([o.reshape(M, D_MODEL)], w_out.astype(BF16), xf, ln_g, ln_b)


def _conv_ffn(xf, xb, w_up, conv_w, conv_b, w_down, layer, ln_g, ln_b):
    h = _ffn_up(xb, w_up, conv_w, conv_b.reshape(conv_b.shape[0], 1, -1), layer)
    return _mm_ln([h], w_down, xf, ln_g, ln_b, bm=256, layer=layer)


def kernel(x, ln_g, ln_b, ev_w_in, ev_mu, ev_w0, ev_w2, ev_a0, ev_a2, ev_g2, ev_k_k, ev_k_a, ev_r_k, ev_lnx_g, ev_lnx_b, ev_v0, ev_v1, ev_v2, ev_conv_w, ev_w_out, od_w_in, od_b_f, od_w_out, ff_w_up, ff_conv_w, ff_conv_b, ff_w_down):
    B, T, D = x.shape
    xf = x.reshape(B * T, D)
    xb = xf.astype(BF16)
    w_down_b = ff_w_down.astype(BF16)
    v_first = None
    for i in range(DEPTH):
        if i % 2 == 0:
            e = i // 2
            vres = (ev_v0[e - 1], ev_v1[e - 1], ev_v2[e - 1]) if e > 0 else None
            xf, xb, v_first = _even_layer(xf, xb, ev_w_in[e], ev_mu[e], ev_w0[e], ev_w2[e], ev_a0[e],
                                          ev_a2[e], ev_g2[e], ev_k_k[e], ev_k_a[e], ev_r_k[e],
                                          ev_lnx_g[e], ev_lnx_b[e], ev_conv_w[e], ev_w_out[e],
                                          v_first, vres, ln_g[i, 0], ln_b[i, 0])
        else:
            o = i // 2
            xf, xb = _odd_layer(xf, xb, od_w_in, o, od_b_f[o], od_w_out[o], ln_g[i, 0], ln_b[i, 0])
        xf, xb = _conv_ffn(xf, xb, ff_w_up, ff_conv_w, ff_conv_b, w_down_b, i,
                           ln_g[i, 1], ln_b[i, 1])
    return xf.reshape(B, T, D)
```
